```python
import math
import jax
import jax.numpy as jnp
from jax import lax
import numpy as np

D_MODEL = 2048
BATCH = 16
SEQ = 2048
DEPTH = 1
DEC_BATCH = 32
DEC_SEQ = 4
PAST_LEN = 16384
PAGE_SIZE = 128

N_HEADS = 16
HEAD_DIM = D_MODEL // N_HEADS
ATTN_WIDTH = N_HEADS * HEAD_DIM
MOBA_BLOCK = 256
MOBA_TOP_K = 3
Q_CHUNK = 8
D_CONV = D_MODEL // 2
CONV_WIDTH = 31
N_BUCKETS = 32
MAX_DISTANCE = 128
N_MEM = 256
XA_HEADS = 4
XA_HEAD_DIM = 128
XA_WIDTH = XA_HEADS * XA_HEAD_DIM
N_GROUPS = 8
EXPERTS_PER_GROUP = 8
N_EXPERTS = N_GROUPS * EXPERTS_PER_GROUP
EXPERT_TOP_K = 2
D_EXPERT = D_MODEL // 4
MOE_ROWS = 128
RMS_EPS = 1e-6
LN_EPS = 1e-5
N_IN = 2 * D_CONV + 3 * ATTN_WIDTH + 2 * D_MODEL
SPLIT_POINTS = (D_CONV, 2 * D_CONV, 2 * D_CONV + ATTN_WIDTH, 2 * D_CONV + 2 * ATTN_WIDTH,
                2 * D_CONV + 3 * ATTN_WIDTH, 2 * D_CONV + 3 * ATTN_WIDTH + D_MODEL)

kernel_name = 'hybrid_conformer_moba_hmoe_decode_step'


def rms_norm(x, g):
    xf = x.astype(jnp.float32)
    y = xf * lax.rsqrt(jnp.mean(xf * xf, axis=-1, keepdims=True) + RMS_EPS)
    return (y * g.astype(jnp.float32)).astype(x.dtype)


def layer_norm(x, g, b):
    xf = x.astype(jnp.float32)
    xc = xf - jnp.mean(xf, axis=-1, keepdims=True)
    var = jnp.mean(xc * xc, axis=-1, keepdims=True)
    return (xc * lax.rsqrt(var + LN_EPS) * g.astype(jnp.float32) + b.astype(jnp.float32)).astype(x.dtype)


def t5_bucket(dist):
    dist = jnp.maximum(dist, 0)
    max_exact = N_BUCKETS // 2
    ratio = jnp.log(jnp.maximum(dist, 1).astype(jnp.float32) / max_exact) / math.log(MAX_DISTANCE / max_exact)
    large = jnp.minimum(max_exact + (ratio * (N_BUCKETS - max_exact)).astype(jnp.int32), N_BUCKETS - 1)
    return jnp.where(dist < max_exact, dist, large)


def mixer_inputs(x, g_mix, w_in):
    h = rms_norm(x, g_mix)
    return jnp.split(h @ w_in, SPLIT_POINTS, axis=-1)


def conformer_conv(glu_a, glu_b, state, w_dw, b_dw, ln_g, ln_b, w_pw):
    u = glu_a * jax.nn.sigmoid(glu_b)
    hist = jnp.concatenate([state.astype(u.dtype), u], axis=1)
    y = lax.conv_general_dilated(hist, w_dw[:, None, :].astype(u.dtype), window_strides=(1,), padding='VALID',
                                 dimension_numbers=('NWC', 'WIO', 'NWC'), feature_group_count=D_CONV)
    y = jax.nn.silu(layer_norm(y + b_dw, ln_g, ln_b))
    return y @ w_pw, hist[:, -(CONV_WIDTH - 1):]


def moba_attend(q, q_pos, k_own, v_own, own_pos, rel_table, sel=None, k_sel=None, v_sel=None, valid=None):
    b, t, h, dh = q.shape
    scale = dh ** -0.5
    own_bias = rel_table[t5_bucket(q_pos[:, None] - own_pos[None, :])]
    own_logits = (jnp.einsum('bthd,bshd->bths', q, k_own, preferred_element_type=jnp.float32) * scale
                  + jnp.transpose(own_bias, (0, 2, 1)).astype(jnp.float32))
    causal = (own_pos[None, :] <= q_pos[:, None])[None, :, None, :]
    own_logits = jnp.where(causal, own_logits, -jnp.inf)
    if sel is None:
        p_own = jax.nn.softmax(own_logits, axis=-1).astype(v_own.dtype)
        return jnp.einsum('bths,bshd->bthd', p_own, v_own)
    n_sel, blk = k_sel.shape[3], k_sel.shape[4]
    key_pos = sel[..., None] * MOBA_BLOCK + jnp.arange(blk)
    head = jnp.arange(h)[None, None, :, None, None]
    sel_bias = rel_table[t5_bucket(q_pos[None, :, None, None, None] - key_pos), head].astype(jnp.float32)
    sel_logits = jnp.einsum('bthd,bthkld->bthkl', q, k_sel, preferred_element_type=jnp.float32) * scale + sel_bias
    sel_logits = jnp.where(valid[..., None], sel_logits, -jnp.inf).reshape(b, t, h, n_sel * blk)
    probs = jax.nn.softmax(jnp.concatenate([sel_logits, own_logits], axis=-1), axis=-1).astype(v_own.dtype)
    p_sel = probs[..., :n_sel * blk].reshape(b, t, h, n_sel, blk)
    p_own = probs[..., n_sel * blk:]
    return jnp.einsum('bthkl,bthkld->bthd', p_sel, v_sel) + jnp.einsum('bths,bshd->bthd', p_own, v_own)


def page_block_means(page_means):
    b, n_pg, h, dh = page_means.shape
    ppb = MOBA_BLOCK // PAGE_SIZE
    return page_means.reshape(b, n_pg // ppb, ppb, h, dh).mean(axis=2)


def moba_prompt(q, k, v, rel_table):
    b, s, h, dh = q.shape
    ppb = MOBA_BLOCK // PAGE_SIZE
    n_cand = (s - 1) // MOBA_BLOCK
    n_top = min(MOBA_TOP_K, n_cand)
    n_blk = -(-s // MOBA_BLOCK)
    pad = n_blk * MOBA_BLOCK - s
    k_blk = jnp.pad(k, ((0, 0), (0, pad), (0, 0), (0, 0))).reshape(b, n_blk, MOBA_BLOCK, h, dh)
    v_blk = jnp.pad(v, ((0, 0), (0, pad), (0, 0), (0, 0))).reshape(b, n_blk, MOBA_BLOCK, h, dh)
    blk_means = None
    if n_top > 0:
        blk_means = page_block_means(jnp.mean(
            k[:, :n_cand * MOBA_BLOCK].reshape(b, n_cand * ppb, PAGE_SIZE, h, dh), axis=2, dtype=jnp.float32))
    n_chunks = s // Q_CHUNK
    q_chunks = jnp.moveaxis(q.reshape(b, n_chunks, Q_CHUNK, h, dh), 1, 0)
    b_idx = jnp.arange(b)[:, None, None, None]
    h_idx = jnp.arange(h)[None, None, :, None]

    def one_chunk(args):
        qc, ci = args
        q_pos = ci * Q_CHUNK + jnp.arange(Q_CHUNK)
        own = (ci * Q_CHUNK) // MOBA_BLOCK
        k_own = lax.dynamic_index_in_dim(k_blk, own, axis=1, keepdims=False)
        v_own = lax.dynamic_index_in_dim(v_blk, own, axis=1, keepdims=False)
        own_pos = own * MOBA_BLOCK + jnp.arange(MOBA_BLOCK)
        if n_top == 0:
            return moba_attend(qc, q_pos, k_own, v_own, own_pos, rel_table)
        scores = jnp.einsum('bthd,bnhd->bthn', qc.astype(jnp.float32), blk_means)
        scores = jnp.where(jnp.arange(n_cand) < own, scores, -jnp.inf)
        _, sel = lax.top_k(scores, n_top)
        valid = sel < own
        k_sel = k_blk[b_idx, sel, :, h_idx, :]
        v_sel = v_blk[b_idx, sel, :, h_idx, :]
        return moba_attend(qc, q_pos, k_own, v_own, own_pos, rel_table, sel, k_sel, v_sel, valid)

    out = lax.map(one_chunk, (q_chunks, jnp.arange(n_chunks)))
    return jnp.moveaxis(out, 0, 1).reshape(b, s, h * dh)


def moba_sample(q, k_new, v_new, cache_k, cache_v, layer, page_means, page_table, rel_table):
    b, t, h, dh = q.shape
    ppb = MOBA_BLOCK // PAGE_SIZE
    n_full = PAST_LEN // MOBA_BLOCK
    rem = PAST_LEN - n_full * MOBA_BLOCK
    q_pos = PAST_LEN + jnp.arange(t)
    own_pages = page_table[:, n_full * ppb: n_full * ppb + rem // PAGE_SIZE]
    k_own = jnp.concatenate([cache_k[layer, own_pages].reshape(b, rem, h, dh).astype(k_new.dtype), k_new], axis=1)
    v_own = jnp.concatenate([cache_v[layer, own_pages].reshape(b, rem, h, dh).astype(v_new.dtype), v_new], axis=1)
    own_pos = jnp.concatenate([n_full * MOBA_BLOCK + jnp.arange(rem), q_pos])
    n_top = min(MOBA_TOP_K, n_full)
    if n_top == 0:
        return moba_attend(q, q_pos, k_own, v_own, own_pos, rel_table).reshape(b, t, h * dh)
    blk_means = page_block_means(page_means[page_table[:, :n_full * ppb]])
    scores = jnp.einsum('bthd,bnhd->bthn', q.astype(jnp.float32), blk_means)
    _, sel = lax.top_k(scores, n_top)
    logical = sel[..., None] * ppb + jnp.arange(ppb)
    phys = page_table[jnp.arange(b)[:, None, None, None, None], logical]
    head = jnp.arange(h)[None, None, :, None, None]
    k_sel = cache_k[layer, phys, :, head, :].reshape(b, t, h, n_top, MOBA_BLOCK, dh).astype(k_new.dtype)
    v_sel = cache_v[layer, phys, :, head, :].reshape(b, t, h, n_top, MOBA_BLOCK, dh).astype(v_new.dtype)
    valid = jnp.ones(sel.shape, dtype=bool)
    out = moba_attend(q, q_pos, k_own, v_own, own_pos, rel_table, sel, k_sel, v_sel, valid)
    return out.reshape(b, t, h * dh)


def merge_branches(x, conv_out, attn_out, gate_c, gate_a, w_out):
    return x + (jax.nn.sigmoid(gate_c) * conv_out + jax.nn.sigmoid(gate_a) * attn_out) @ w_out


def memory_kv(mem, g_mem, w_xk, w_xv):
    b = mem.shape[0]
    hm = rms_norm(mem, g_mem)
    return ((hm @ w_xk).reshape(b, N_MEM, XA_HEADS, XA_HEAD_DIM),
            (hm @ w_xv).reshape(b, N_MEM, XA_HEADS, XA_HEAD_DIM))


def cross_attention(h, mem_k, mem_v, w_xq, w_xo):
    b, t, _ = h.shape
    q = (h @ w_xq).reshape(b, t, XA_HEADS, XA_HEAD_DIM)
    s = jnp.einsum('bthd,bmhd->bhtm', q, mem_k.astype(q.dtype), preferred_element_type=jnp.float32) * (XA_HEAD_DIM ** -0.5)
    p = jax.nn.softmax(s, axis=-1).astype(q.dtype)
    o = jnp.einsum('bhtm,bmhd->bthd', p, mem_v.astype(q.dtype)).reshape(b, t, XA_WIDTH)
    return o @ w_xo


def hier_moe(h, w_rg, b_rg, w_re, b_re, w1, w3, w2):
    b, t, d = h.shape
    n_tok = b * t
    xf = h.reshape(n_tok, d)
    g_logits = jnp.matmul(xf, w_rg, preferred_element_type=jnp.float32) + b_rg.astype(jnp.float32)
    g_prob = jax.nn.softmax(g_logits, axis=-1)
    g_top = jnp.argmax(g_logits, axis=-1)
    e_logits = (jnp.matmul(xf, w_re, preferred_element_type=jnp.float32) + b_re.astype(jnp.float32)
                ).reshape(n_tok, N_GROUPS, EXPERTS_PER_GROUP)
    e_in = jnp.take_along_axis(e_logits, g_top[:, None, None], axis=1)[:, 0]
    e_val, e_idx = lax.top_k(e_in, EXPERT_TOP_K)
    gate = jnp.take_along_axis(g_prob, g_top[:, None], axis=1) * jax.nn.softmax(e_val, axis=-1)
    expert = (g_top[:, None] * EXPERTS_PER_GROUP + e_idx).reshape(-1).astype(jnp.int32)
    n_as = n_tok * EXPERT_TOP_K
    order = jnp.argsort(expert)
    sizes = jnp.bincount(expert, length=N_EXPERTS)
    starts = jnp.cumsum(sizes) - sizes
    padded = (sizes + MOE_ROWS - 1) // MOE_ROWS * MOE_ROWS
    pad_end = jnp.cumsum(padded)
    pad_start = pad_end - padded
    sorted_e = expert[order]
    slot_sorted = (pad_start[sorted_e] + jnp.arange(n_as) - starts[sorted_e]).astype(jnp.int32)
    slot = jnp.zeros((n_as,), jnp.int32).at[order].set(slot_sorted)
    n_blocks = -(-n_as // MOE_ROWS) + N_EXPERTS
    tok = jnp.arange(n_as) // EXPERT_TOP_K
    x_pad = jnp.zeros((n_blocks * MOE_ROWS, d), h.dtype).at[slot].set(xf[tok])
    blk_expert = jnp.minimum(jnp.searchsorted(pad_end, jnp.arange(n_blocks) * MOE_ROWS, side='right'), N_EXPERTS - 1)

    def expert_rows(args):
        xb, e = args
        return (jax.nn.silu(xb @ w1[e]) * (xb @ w3[e])) @ w2[e]

    y_pad = lax.map(expert_rows, (x_pad.reshape(n_blocks, MOE_ROWS, d), blk_expert)).reshape(n_blocks * MOE_ROWS, d)
    y_as = y_pad[slot].reshape(n_tok, EXPERT_TOP_K, d)
    return jnp.einsum('nk,nkd->nd', gate.astype(y_as.dtype), y_as).reshape(b, t, d)


def channel_stage(x, mem_k, mem_v, g_xattn, w_xq, w_xo, g_moe, w_rg, b_rg, w_re, b_re, w1, w3, w2):
    x = x + cross_attention(rms_norm(x, g_xattn), mem_k, mem_v, w_xq, w_xo)
    return x + hier_moe(rms_norm(x, g_moe), w_rg, b_rg, w_re, b_re, w1, w3, w2)


def setup_inputs(seed: int = 0) -> dict:
    key = jax.random.key(seed)
    keys = iter(jax.random.split(key, 40))

    def nrm(shape, scale=1.0):
        return jax.random.normal(next(keys), shape, jnp.float32) * scale

    def gain(shape):
        return 1.0 + nrm(shape, 0.1)

    n_pages = PAST_LEN // PAGE_SIZE
    n_phys = (DEC_BATCH * n_pages * 5) // 4
    kv_shape = (DEPTH, n_phys, PAGE_SIZE, N_HEADS, HEAD_DIM)
    mem_shape = (DEPTH, DEC_BATCH, N_MEM, XA_HEADS, XA_HEAD_DIM)
    return {
        'x_prompt': nrm((BATCH, SEQ, D_MODEL)),
        'x_sample': nrm((DEC_BATCH, DEC_SEQ, D_MODEL)),
        'cache_k': nrm(kv_shape),
        'cache_v': nrm(kv_shape),
        'cache_conv': nrm((DEPTH, DEC_BATCH, CONV_WIDTH - 1, D_CONV), 0.5),
        'cache_mem_k': nrm(mem_shape),
        'cache_mem_v': nrm(mem_shape),
        'page_table': jax.random.permutation(next(keys), n_phys)[:DEC_BATCH * n_pages].reshape(DEC_BATCH, n_pages).astype(jnp.int32),
        'mem_prompt': nrm((BATCH, N_MEM, D_MODEL)),
        'rel_table': nrm((N_BUCKETS, N_HEADS), 0.5),
        'g_mix': gain((DEPTH, D_MODEL)),
        'w_in': nrm((DEPTH, D_MODEL, N_IN), D_MODEL ** -0.5),
        'w_dw': nrm((DEPTH, CONV_WIDTH, D_CONV), CONV_WIDTH ** -0.5),
        'b_dw': nrm((DEPTH, D_CONV), 0.02),
        'ln_g': gain((DEPTH, D_CONV)),
        'ln_b': nrm((DEPTH, D_CONV), 0.02),
        'w_pw': nrm((DEPTH, D_CONV, D_MODEL), D_CONV ** -0.5),
        'w_out': nrm((DEPTH, D_MODEL, D_MODEL), D_MODEL ** -0.5),
        'g_mem': gain((DEPTH, D_MODEL)),
        'w_xk': nrm((DEPTH, D_MODEL, XA_WIDTH), D_MODEL ** -0.5),
        'w_xv': nrm((DEPTH, D_MODEL, XA_WIDTH), D_MODEL ** -0.5),
        'g_xattn': gain((DEPTH, D_MODEL)),
        'w_xq': nrm((DEPTH, D_MODEL, XA_WIDTH), D_MODEL ** -0.5),
        'w_xo': nrm((DEPTH, XA_WIDTH, D_MODEL), XA_WIDTH ** -0.5),
        'g_moe': gain((DEPTH, D_MODEL)),
        'w_rg': nrm((DEPTH, D_MODEL, N_GROUPS), D_MODEL ** -0.5),
        'b_rg': nrm((DEPTH, N_GROUPS), 0.01),
        'w_re': nrm((DEPTH, D_MODEL, N_EXPERTS), D_MODEL ** -0.5),
        'b_re': nrm((DEPTH, N_EXPERTS), 0.01),
        'w1': nrm((DEPTH, N_EXPERTS, D_MODEL, D_EXPERT), D_MODEL ** -0.5),
        'w3': nrm((DEPTH, N_EXPERTS, D_MODEL, D_EXPERT), D_MODEL ** -0.5),
        'w2': nrm((DEPTH, N_EXPERTS, D_EXPERT, D_MODEL), D_EXPERT ** -0.5),
        'g_final': gain((D_MODEL,)),
    }


def reference(x_prompt, x_sample, cache_k, cache_v, cache_conv, cache_mem_k, cache_mem_v, page_table, mem_prompt,
              rel_table, g_mix, w_in, w_dw, b_dw, ln_g, ln_b, w_pw, w_out, g_mem, w_xk, w_xv, g_xattn, w_xq, w_xo,
              g_moe, w_rg, b_rg, w_re, b_re, w1, w3, w2, g_final):
    page_means_k = jnp.mean(cache_k, axis=2, dtype=jnp.float32)
    xp, xs = x_prompt, x_sample
    bp, s_len, _ = xp.shape
    bs, t_len, _ = xs.shape
    kp_l, vp_l, convp_l, mkp_l, mvp_l = [], [], [], [], []
    ks_l, vs_l, convs_l = [], [], []
    for l in range(DEPTH):
        ga_, gb_, q, k, v, gc, gat = mixer_inputs(xp, g_mix[l], w_in[l])
        conv_out, conv_p = conformer_conv(ga_, gb_, jnp.zeros((bp, CONV_WIDTH - 1, D_CONV), xp.dtype),
                                          w_dw[l], b_dw[l], ln_g[l], ln_b[l], w_pw[l])
        q = q.reshape(bp, s_len, N_HEADS, HEAD_DIM)
        k = k.reshape(bp, s_len, N_HEADS, HEAD_DIM)
        v = v.reshape(bp, s_len, N_HEADS, HEAD_DIM)
        attn = moba_prompt(q, k, v, rel_table)
        xp = merge_branches(xp, conv_out, attn, gc, gat, w_out[l])
        mk, mv = memory_kv(mem_prompt, g_mem[l], w_xk[l], w_xv[l])
        xp = channel_stage(xp, mk, mv, g_xattn[l], w_xq[l], w_xo[l], g_moe[l], w_rg[l], b_rg[l], w_re[l], b_re[l],
                           w1[l], w3[l], w2[l])
        kp_l.append(k)
        vp_l.append(v)
        convp_l.append(conv_p)
        mkp_l.append(mk)
        mvp_l.append(mv)
        ga_, gb_, q, k, v, gc, gat = mixer_inputs(xs, g_mix[l], w_in[l])
        conv_out, conv_s = conformer_conv(ga_, gb_, cache_conv[l], w_dw[l], b_dw[l], ln_g[l], ln_b[l], w_pw[l])
        q = q.reshape(bs, t_len, N_HEADS, HEAD_DIM)
        k = k.reshape(bs, t_len, N_HEADS, HEAD_DIM)
        v = v.reshape(bs, t_len, N_HEADS, HEAD_DIM)
        attn = moba_sample(q, k, v, cache_k, cache_v, l, page_means_k[l], page_table, rel_table)
        xs = merge_branches(xs, conv_out, attn, gc, gat, w_out[l])
        xs = channel_stage(xs, cache_mem_k[l], cache_mem_v[l], g_xattn[l], w_xq[l], w_xo[l], g_moe[l], w_rg[l],
                           b_rg[l], w_re[l], b_re[l], w1[l], w3[l], w2[l])
        ks_l.append(k)
        vs_l.append(v)
        convs_l.append(conv_s)
    y_prompt = rms_norm(xp, g_final)
    y_sample = rms_norm(xs, g_final)
    return (y_prompt, y_sample, jnp.stack(kp_l), jnp.stack(vp_l), jnp.stack(convp_l), jnp.stack(mkp_l),
            jnp.stack(mvp_l), jnp.stack(ks_l), jnp.stack(vs_l), jnp.stack(convs_l))
```

```python
import functools
import math

import numpy as np
import jax
import jax.numpy as jnp
from jax import lax
from jax.experimental import pallas as pl
from jax.experimental.pallas import tpu as pltpu

f32 = jnp.float32
bf16 = jnp.bfloat16
i32 = jnp.int32

MOBA_BLOCK = 256
MOBA_TOP_K = 3
MAX_DISTANCE = 128
EXPERT_TOP_K = 2
RMS_EPS = 1e-6
LN_EPS = 1e-5

LANES = 128
SAMPLE_ROWS = 16
MASKED = -1e30
BIG_I = 1 << 20
VMEM_LIMIT = 56 * 1024 * 1024


def _params(sem, vmem=VMEM_LIMIT):
    return pltpu.CompilerParams(dimension_semantics=sem, vmem_limit_bytes=vmem)


def _sigmoid(x):
    return 1.0 / (1.0 + jnp.exp(-x))


def _rms(x, g, eps=RMS_EPS):
    return x * lax.rsqrt(jnp.mean(x * x, axis=-1, keepdims=True) + eps) * g


def _nt(a, b, **kw):
    return lax.dot_general(a, b, (((1,), (1,)), ((), ())), preferred_element_type=f32, **kw)


def _t5_bucket_np(dist, n_buckets):
    dist = np.maximum(dist, 0)
    max_exact = n_buckets // 2
    ratio = (np.log(np.maximum(dist, 1).astype(np.float32) / np.float32(max_exact))
             / np.float32(math.log(MAX_DISTANCE / max_exact))).astype(np.float32)
    large = np.minimum(max_exact + (ratio * np.float32(n_buckets - max_exact)).astype(np.int32), n_buckets - 1)
    return np.where(dist < max_exact, dist, large).astype(np.int32)


def _norm_proj_body(x_ref, g_ref, *refs, n_out):
    w_refs, o_refs, hn_ref = refs[:n_out], refs[n_out:2 * n_out], refs[2 * n_out]

    @pl.when(pl.program_id(1) == 0)
    def _():
        hn_ref[...] = _rms(x_ref[...], g_ref[...]).astype(bf16)

    h = hn_ref[...]
    for w_ref, o_ref in zip(w_refs, o_refs):
        o_ref[...] = jnp.dot(h, w_ref[...], preferred_element_type=f32).astype(o_ref.dtype)


def _norm_proj(x, g, w, pieces, tm):
    m, k = x.shape
    tm = min(tm, m)
    assert m % tm == 0
    min_w = min(wd for _, wd in pieces)
    nj = max(1, min_w // 256)
    in_specs = [pl.BlockSpec((tm, k), lambda i, j: (i, 0)), pl.BlockSpec((1, k), lambda i, j: (0, 0))]
    out_specs, out_shape = [], []
    for off, wd in pieces:
        assert wd % nj == 0 and off % (wd // nj) == 0
        tn = wd // nj
        in_specs.append(pl.BlockSpec((k, tn), functools.partial(lambda i, j, o: (0, o + j), o=off // tn)))
        out_specs.append(pl.BlockSpec((tm, tn), lambda i, j: (i, j)))
        out_shape.append(jax.ShapeDtypeStruct((m, wd), f32))
    return pl.pallas_call(
        functools.partial(_norm_proj_body, n_out=len(pieces)),
        grid=(m // tm, nj), in_specs=in_specs, out_specs=out_specs, out_shape=out_shape,
        scratch_shapes=[pltpu.VMEM((tm, k), bf16)],
        compiler_params=_params(("parallel", "arbitrary")), name="norm_proj",
    )(x, g.reshape(1, k), *([w] * len(pieces)))


HALO = 32


def _conv_body(glu_ref, halo_ref, gc_ref, wdw_ref, bdw_ref, lng_ref, lnb_ref, wpw_ref, cg_ref, st_ref,
               hist_ref, y_ref, *, dc, width, ts, t_real, halo_is_state, n_tiles):
    i = pl.program_id(1)
    glu = glu_ref[0]
    hist_ref[HALO:HALO + ts, :] = glu[:, :dc] * _sigmoid(glu[:, dc:])
    if halo_is_state:
        hist_ref[0:HALO, :] = halo_ref[0]
    else:
        hl = halo_ref[0]
        uh = hl[:, :dc] * _sigmoid(hl[:, dc:])
        hist_ref[0:HALO, :] = jnp.where(i == 0, 0.0, uh)
    base = HALO - (width - 1)
    for c in range(dc // LANES):
        cs = slice(c * LANES, (c + 1) * LANES)
        acc = jnp.zeros((ts, LANES), f32)
        for w in range(width):
            acc = acc + hist_ref[base + w:base + w + ts, cs] * wdw_ref[w:w + 1, cs]
        y_ref[:, cs] = acc + bdw_ref[:, cs]
    y = y_ref[...]
    yc = y - jnp.mean(y, axis=-1, keepdims=True)
    var = jnp.mean(yc * yc, axis=-1, keepdims=True)
    z = yc * lax.rsqrt(var + LN_EPS) * lng_ref[...] + lnb_ref[...]
    z = z * _sigmoid(z)
    conv_out = jnp.dot(z.astype(bf16), wpw_ref[...], preferred_element_type=f32)
    cg_ref[0] = (_sigmoid(gc_ref[0]) * conv_out).astype(cg_ref.dtype)

    @pl.when(i == n_tiles - 1)
    def _():
        last = HALO + (ts if t_real is None else t_real)
        st_ref[0] = hist_ref[last - (width - 1):last, :]


def _conv_branch(glu, gc, state, w_dw, b_dw, ln_g, ln_b, w_pw, *, ts, t_real=None):
    b, t, _ = glu.shape
    width, dc = w_dw.shape
    d = w_pw.shape[1]
    ts = min(ts, t)
    assert t % ts == 0 and ts % HALO == 0 or t == ts
    assert width - 1 <= HALO and dc % LANES == 0
    n_tiles = t // ts
    halo_is_state = state is not None
    if halo_is_state:
        assert n_tiles == 1
        halo_arr = state
        halo_spec = pl.BlockSpec((1, HALO, dc), lambda bi, i: (bi, 0, 0))
    else:
        assert t >= width - 1
        halo_arr = glu
        r = ts // HALO
        halo_spec = pl.BlockSpec((1, HALO, 2 * dc), lambda bi, i: (bi, jnp.maximum(i * r - 1, 0), 0))
    const = lambda bi, i: (0, 0)
    return pl.pallas_call(
        functools.partial(_conv_body, dc=dc, width=width, ts=ts, t_real=t_real, halo_is_state=halo_is_state,
                          n_tiles=n_tiles),
        grid=(b, n_tiles),
        in_specs=[pl.BlockSpec((1, ts, 2 * dc), lambda bi, i: (bi, i, 0)), halo_spec,
                  pl.BlockSpec((1, ts, d), lambda bi, i: (bi, i, 0)),
                  pl.BlockSpec((width, dc), const), pl.BlockSpec((1, dc), const), pl.BlockSpec((1, dc), const),
                  pl.BlockSpec((1, dc), const), pl.BlockSpec((dc, d), const)],
        out_specs=[pl.BlockSpec((1, ts, d), lambda bi, i: (bi, i, 0)),
                   pl.BlockSpec((1, width - 1, dc), lambda bi, i: (bi, 0, 0))],
        out_shape=[jax.ShapeDtypeStruct((b, t, d), bf16), jax.ShapeDtypeStruct((b, width - 1, dc), f32)],
        scratch_shapes=[pltpu.VMEM((HALO + ts, dc), f32), pltpu.VMEM((ts, dc), f32)],
        compiler_params=_params(("parallel", "arbitrary")), name="conv_branch",
    )(glu, halo_arr, gc, w_dw, b_dw.reshape(1, dc), ln_g.reshape(1, dc), ln_b.reshape(1, dc), w_pw)


def _bias_body(tab_ref, map_ref, o_ref, *, n_buckets):
    h = pl.program_id(0)
    bm = map_ref[...]
    acc = jnp.zeros(bm.shape, f32)
    for bk in range(n_buckets):
        acc = jnp.where(bm == bk, tab_ref[bk, h], acc)
    o_ref[0] = acc


def _bias_tiles(rel_table, bucket_map):
    n_buckets, n_heads = rel_table.shape
    r, c = bucket_map.shape
    return pl.pallas_call(
        functools.partial(_bias_body, n_buckets=n_buckets),
        grid=(n_heads,),
        in_specs=[pl.BlockSpec(memory_space=pltpu.SMEM), pl.BlockSpec((r, c), lambda h: (0, 0))],
        out_specs=pl.BlockSpec((1, r, c), lambda h: (h, 0, 0)),
        out_shape=jax.ShapeDtypeStruct((n_heads, r, c), f32),
        compiler_params=_params(("parallel",)), name="bias_tiles",
    )(rel_table, jnp.asarray(bucket_map))


def _moba_prompt_body(tab_ref, q_ref, k_ref, v_ref, ga_ref, bias_ref, o_ref, vt_ref, sel_ref, *,
                      s_len, dh, page, n_buckets):
    blk = MOBA_BLOCK
    nb = s_len // blk
    ppb = blk // page
    h = pl.program_id(1)
    scale = dh ** -0.5
    far_bias = tab_ref[n_buckets - 1, h]

    rows = []
    for n in range(nb):
        acc = jnp.zeros((1, dh), f32)
        for p in range(ppb):
            acc = acc + jnp.sum(k_ref[0, n * blk + p * page:n * blk + (p + 1) * page, :], axis=0,
                                keepdims=True) * (1.0 / page)
        rows.append(acc * (1.0 / ppb))
        vt_ref[n] = jnp.transpose(v_ref[0, n * blk:(n + 1) * blk, :]).astype(bf16)
    bmean = jnp.concatenate(rows, axis=0)
    sc = _nt(bmean, q_ref[0], precision=lax.Precision.HIGHEST)
    n_idx = lax.broadcasted_iota(i32, (nb, s_len), 0)
    own = lax.broadcasted_iota(i32, (nb, s_len), 1) // blk
    valid = n_idx < own
    scm = jnp.where(valid, sc, -jnp.inf)
    rank = jnp.zeros((nb, s_len), f32)
    for m in range(nb - 1):
        row = scm[m:m + 1, :]
        beats = (row > scm) | ((row == scm) & (m < n_idx))
        rank = rank + jnp.where(beats, 1.0, 0.0)
    sel = jnp.where(valid & (rank < MOBA_TOP_K), 1.0, 0.0)
    for j in range(nb):
        sel_ref[j] = sel[:, j * blk:(j + 1) * blk]

    kk = lax.broadcasted_iota(i32, (blk, blk), 0)
    qq = lax.broadcasted_iota(i32, (blk, blk), 1)
    causal = kk <= qq

    def tile(n, qj, bias, mask, m_i, l_i, acc):
        start = n * blk if isinstance(n, int) else pl.multiple_of(n * blk, blk)
        kn = k_ref[0, pl.ds(start, blk), :].astype(bf16)
        s = _nt(kn, qj) * scale + bias
        s = jnp.where(mask, s, MASKED)
        m_new = jnp.maximum(m_i, jnp.max(s, axis=0, keepdims=True))
        alpha = jnp.exp(m_i - m_new)
        p = jnp.exp(s - m_new)
        l_new = alpha * l_i + jnp.sum(p, axis=0, keepdims=True)
        acc_new = alpha * acc + jnp.dot(vt_ref[n], p.astype(bf16), preferred_element_type=f32)
        return m_new, l_new, acc_new

    for j in range(nb):
        qj = q_ref[0, j * blk:(j + 1) * blk, :].astype(bf16)
        m_i = jnp.full((1, blk), MASKED, f32)
        l_i = jnp.zeros((1, blk), f32)
        acc = jnp.zeros((dh, blk), f32)
        m_i, l_i, acc = tile(j, qj, bias_ref[0, 0:blk, :], causal, m_i, l_i, acc)
        if j >= 1:
            m_i, l_i, acc = tile(j - 1, qj, bias_ref[0, blk:2 * blk, :], sel_ref[j, j - 1:j, :] > 0.5, m_i, l_i, acc)
        if j >= 2:
            def far_tile(n, carry, qj=qj, j=j):
                return tile(n, qj, far_bias, sel_ref[j, pl.ds(n, 1), :] > 0.5, *carry)
            m_i, l_i, acc = lax.fori_loop(0, j - 1, far_tile, (m_i, l_i, acc))
        out = jnp.transpose(acc * (1.0 / l_i))
        gate = _sigmoid(ga_ref[0, j * blk:(j + 1) * blk, :])
        o_ref[0, j * blk:(j + 1) * blk, :] = (gate * out).astype(o_ref.dtype)


def _moba_prompt(q, k, v, ga, rel_table, page):
    b, s_len, width = q.shape
    n_buckets, n_heads = rel_table.shape
    dh = width // n_heads
    blk = MOBA_BLOCK
    assert s_len % blk == 0 and blk % page == 0 and dh % LANES == 0
    assert blk + 1 >= MAX_DISTANCE, "blocks two or more away must share the last bucket"
    kk = np.arange(blk)[:, None]
    qq = np.arange(blk)[None, :]
    bmap = np.concatenate([_t5_bucket_np(qq - kk, n_buckets), _t5_bucket_np(qq - kk + blk, n_buckets)], axis=0)
    bias = _bias_tiles(rel_table, bmap)
    nb = s_len // blk
    qkv_spec = pl.BlockSpec((1, s_len, dh), lambda bi, hi: (bi, 0, hi))
    return pl.pallas_call(
        functools.partial(_moba_prompt_body, s_len=s_len, dh=dh, page=page, n_buckets=n_buckets),
        grid=(b, n_heads),
        in_specs=[pl.BlockSpec(memory_space=pltpu.SMEM), qkv_spec, qkv_spec, qkv_spec, qkv_spec,
                  pl.BlockSpec((1, 2 * blk, blk), lambda bi, hi: (hi, 0, 0))],
        out_specs=pl.BlockSpec((1, s_len, dh), lambda bi, hi: (bi, 0, hi)),
        out_shape=jax.ShapeDtypeStruct((b, s_len, width), bf16),
        scratch_shapes=[pltpu.VMEM((nb, dh, blk), bf16), pltpu.VMEM((nb, nb, blk), f32)],
        compiler_params=_params(("parallel", "parallel")), name="moba_prompt",
    )(rel_table, q, k, v, ga, bias)


def _page_means_body(pt_ref, *refs, n_in, page, ppb):
    page_refs, o_ref = refs[:n_in], refs[n_in]
    for blk_i in range(n_in // ppb):
        acc = jnp.zeros(o_ref.shape[2:], f32)
        for p in range(ppb):
            acc = acc + jnp.sum(page_refs[blk_i * ppb + p][...], axis=0) * (1.0 / page)
        o_ref[0, blk_i] = acc * (1.0 / ppb)


def _block_means(cache_k, layer, page_table, n_blocks):
    _, _, page, n_heads, dh = cache_k.shape
    bs, n_pages = page_table.shape
    ppb = MOBA_BLOCK // page
    bps = 2 if n_blocks % 2 == 0 else 1
    n_in = bps * ppb

    def page_spec(p):
        return pl.BlockSpec((None, None, page, n_heads, dh),
                            lambda b, c, pt: (layer, pt[b * n_pages + c * n_in + p], 0, 0, 0))

    return pl.pallas_call(
        functools.partial(_page_means_body, n_in=n_in, page=page, ppb=ppb),
        grid_spec=pltpu.PrefetchScalarGridSpec(
            num_scalar_prefetch=1, grid=(bs, n_blocks // bps),
            in_specs=[page_spec(p) for p in range(n_in)],
            out_specs=pl.BlockSpec((1, bps, n_heads, dh), lambda b, c, pt: (b, c, 0, 0))),
        out_shape=jax.ShapeDtypeStruct((bs, n_blocks, n_heads, dh), f32),
        compiler_params=_params(("parallel", "parallel")), name="block_means",
    )(page_table.reshape(-1), *([cache_k] * n_in))


def _select_body(q_ref, bm_ref, o_ref, *, n_top):
    sc = _nt(q_ref[0], bm_ref[0], precision=lax.Precision.HIGHEST)
    lane = lax.broadcasted_iota(i32, sc.shape, 1)
    out_lane = lax.broadcasted_iota(i32, o_ref.shape[2:], 1)
    out = jnp.zeros(o_ref.shape[2:], i32)
    for kq in range(n_top):
        best = jnp.max(sc, axis=-1, keepdims=True)
        idx = jnp.min(jnp.where(sc == best, lane, BIG_I), axis=-1, keepdims=True)
        out = jnp.where(out_lane == kq, idx, out)
        sc = jnp.where(lane == idx, -jnp.inf, sc)
    o_ref[0, 0] = out


def _select_blocks(q, bmeans, n_heads, n_top):
    bs, rows, width = q.shape
    dh = width // n_heads
    n_blocks = bmeans.shape[1]
    return pl.pallas_call(
        functools.partial(_select_body, n_top=n_top),
        grid=(bs, n_heads),
        in_specs=[pl.BlockSpec((1, rows, dh), lambda b, h: (b, 0, h)),
                  pl.BlockSpec((1, n_blocks, dh), lambda b, h: (b, 0, h))],
        out_specs=pl.BlockSpec((1, 1, rows, LANES), lambda b, h: (b, h, 0, 0)),
        out_shape=jax.ShapeDtypeStruct((bs, n_heads, rows, LANES), i32),
        compiler_params=_params(("parallel", "parallel")), name="select_blocks",
    )(q, bmeans)


def _moba_sample_body(phys_ref, sel_ref, tab_ref, ck_ref, cv_ref, q_ref, kn_ref, vn_ref, ga_ref, adj_ref, own_ref,
                      o_ref, kbuf, vbuf, sem, *, layer, t_len, n_top, ppb, page, n_heads, n_full, n_buckets, dh):
    b, h = pl.program_id(0), pl.program_id(1)
    blk = ppb * page
    scale = dh ** -0.5
    far_bias = tab_ref[n_buckets - 1, h]
    n_pg = n_top * ppb

    def copies(t, c):
        base = ((b * t_len + t) * n_heads + h) * n_pg + c
        pg = phys_ref[base]
        dst = pl.ds(c * page, page)
        return (pltpu.make_async_copy(ck_ref.at[layer, pg, :, h, :], kbuf.at[t, dst, :], sem.at[0]),
                pltpu.make_async_copy(cv_ref.at[layer, pg, :, h, :], vbuf.at[t, dst, :], sem.at[1]))

    for t in range(t_len):
        for c in range(n_pg):
            ck, cv = copies(t, c)
            ck.start()
            cv.start()
    for t in range(t_len):
        for c in range(n_pg):
            ck, cv = copies(t, c)
            ck.wait()
            cv.wait()

    rows = q_ref.shape[1]
    row_id = lax.broadcasted_iota(i32, (rows, 1), 0)
    out = jnp.zeros((rows, dh), f32)
    kn = kn_ref[0]
    vn = vn_ref[0]
    for t in range(t_len):
        qt = q_ref[0, t:t + 1, :]
        s_sel = jnp.sum(kbuf[t] * qt, axis=-1, keepdims=True) * scale
        biases = []
        for c in range(n_top):
            sblk = sel_ref[((b * t_len + t) * n_heads + h) * n_top + c]
            biases.append(jnp.where(sblk == n_full - 1, adj_ref[0, t * blk:(t + 1) * blk, :], far_bias))
        s_sel = s_sel + jnp.concatenate(biases, axis=0)
        s_own = jnp.sum(kn * qt, axis=-1, keepdims=True) * scale + own_ref[0, t * rows:(t + 1) * rows, :]
        s_own = jnp.where(row_id <= t, s_own, MASKED)
        m = jnp.maximum(jnp.max(s_sel, axis=0, keepdims=True), jnp.max(s_own, axis=0, keepdims=True))
        p_sel = jnp.exp(s_sel - m)
        p_own = jnp.exp(s_own - m)
        l = jnp.sum(p_sel, axis=0, keepdims=True) + jnp.sum(p_own, axis=0, keepdims=True)
        o_t = (jnp.sum(p_sel * vbuf[t], axis=0, keepdims=True) + jnp.sum(p_own * vn, axis=0, keepdims=True)) / l
        out = jnp.where(row_id == t, o_t, out)
    o_ref[0] = (_sigmoid(ga_ref[0]) * out).astype(o_ref.dtype)


def _moba_sample(q, k_new, v_new, ga, cache_k, cache_v, layer, page_table, rel_table, t_len):
    bs, rows, width = q.shape
    _, _, page, n_heads, dh = cache_k.shape
    n_buckets = rel_table.shape[0]
    n_pages = page_table.shape[1]
    blk = MOBA_BLOCK
    ppb = blk // page
    past = n_pages * page
    n_full = past // blk
    assert past % blk == 0, "the new tokens must start a fresh MoBA block"
    assert t_len <= rows and t_len <= blk // 2 and blk + 1 >= MAX_DISTANCE
    n_top = min(MOBA_TOP_K, n_full)
    assert n_top >= 1
    bmeans = _block_means(cache_k, layer, page_table, n_full)
    sel = _select_blocks(q, bmeans.reshape(bs, n_full, width), n_heads, n_top)
    sel = jnp.transpose(sel[:, :, :t_len, :n_top], (0, 2, 1, 3))
    logical = sel[..., None] * ppb + jnp.arange(ppb, dtype=i32)
    phys = jnp.take_along_axis(page_table[:, None, None, :], logical.reshape(bs, t_len, n_heads, n_top * ppb),
                               axis=-1)
    tt = np.arange(8)[:, None]
    adj_map = _t5_bucket_np(blk + tt - np.arange(blk)[None, :], n_buckets)
    own_map = _t5_bucket_np(tt - np.arange(LANES)[None, :], n_buckets)
    tiles = _bias_tiles(rel_table, np.concatenate([adj_map, own_map], axis=1))
    adj = tiles[:, :t_len, :blk].reshape(n_heads, t_len * blk, 1)
    own = tiles[:, :t_len, blk:blk + rows].reshape(n_heads, t_len * rows, 1)
    tok_spec = pl.BlockSpec((1, rows, dh), lambda b, h, *_: (b, 0, h))
    return pl.pallas_call(
        functools.partial(_moba_sample_body, layer=layer, t_len=t_len, n_top=n_top, ppb=ppb, page=page,
                          n_heads=n_heads, n_full=n_full, n_buckets=n_buckets, dh=dh),
        grid_spec=pltpu.PrefetchScalarGridSpec(
            num_scalar_prefetch=2, grid=(bs, n_heads),
            in_specs=[pl.BlockSpec(memory_space=pltpu.SMEM), pl.BlockSpec(memory_space=pl.ANY),
                      pl.BlockSpec(memory_space=pl.ANY), tok_spec, tok_spec, tok_spec, tok_spec,
                      pl.BlockSpec((1, t_len * blk, 1), lambda b, h, *_: (h, 0, 0)),
                      pl.BlockSpec((1, t_len * rows, 1), lambda b, h, *_: (h, 0, 0))],
            out_specs=pl.BlockSpec((1, rows, dh), lambda b, h, *_: (b, 0, h)),
            scratch_shapes=[pltpu.VMEM((t_len, n_top * blk, dh), f32), pltpu.VMEM((t_len, n_top * blk, dh), f32),
                            pltpu.SemaphoreType.DMA((2,))]),
        out_shape=jax.ShapeDtypeStruct((bs, rows, width), bf16),
        compiler_params=_params(("arbitrary", "arbitrary")), name="moba_sample",
    )(phys.reshape(-1).astype(i32), sel.reshape(-1).astype(i32), rel_table, cache_k, cache_v, q, k_new, v_new, ga,
      adj, own)


def _token_body(cg_ref, ag_ref, x_ref, mk_ref, mv_ref, wout_ref, gx_ref, wxq_ref, wxo_ref, gm_ref, wrh_ref, wrl_ref,
                br_ref, x2_ref, h2_ref, lg_ref, *, xa_heads, xa_dh):
    merged = (cg_ref[...].astype(f32) + ag_ref[...].astype(f32)).astype(bf16)
    x1 = x_ref[...] + jnp.dot(merged, wout_ref[...], preferred_element_type=f32)
    hx = _rms(x1, gx_ref[...]).astype(bf16)
    qx = jnp.dot(hx, wxq_ref[...], preferred_element_type=f32)
    scale = xa_dh ** -0.5
    outs = []
    for hd in range(xa_heads):
        cs = slice(hd * xa_dh, (hd + 1) * xa_dh)
        s = _nt(qx[:, cs].astype(bf16), mk_ref[0, :, cs].astype(bf16)) * scale
        p = jnp.exp(s - jnp.max(s, axis=-1, keepdims=True))
        p = p / jnp.sum(p, axis=-1, keepdims=True)
        outs.append(jnp.dot(p.astype(bf16), mv_ref[0, :, cs].astype(bf16), preferred_element_type=f32))
    o = jnp.concatenate(outs, axis=-1).astype(bf16)
    x2 = x1 + jnp.dot(o, wxo_ref[...], preferred_element_type=f32)
    x2_ref[...] = x2
    h2 = _rms(x2, gm_ref[...])
    h2_ref[...] = h2
    hi = h2.astype(bf16)
    lo = (h2 - hi.astype(f32)).astype(bf16)
    lg_ref[...] = (jnp.dot(hi, wrh_ref[...], preferred_element_type=f32)
                   + jnp.dot(lo, wrh_ref[...], preferred_element_type=f32)
                   + jnp.dot(hi, wrl_ref[...], preferred_element_type=f32) + br_ref[...])


def _token_stage(cg, ag, x, mem_k, mem_v, w_out, g_x, w_xq, w_xo, g_moe, wr_hi, wr_lo, b_r, *, tm, rows_per_seq,
                 xa_heads):
    n, d = x.shape
    xa_w = w_xq.shape[1]
    n_mem = mem_k.shape[1]
    tm = min(tm, rows_per_seq)
    assert rows_per_seq % tm == 0 and n % tm == 0
    per = rows_per_seq // tm
    row = lambda i: (i, 0)
    const = lambda i: (0, 0)
    mem_spec = pl.BlockSpec((1, n_mem, xa_w), lambda i: (i // per, 0, 0))
    nl = wr_hi.shape[1]
    return pl.pallas_call(
        functools.partial(_token_body, xa_heads=xa_heads, xa_dh=xa_w // xa_heads),
        grid=(n // tm,),
        in_specs=[pl.BlockSpec((tm, d), row), pl.BlockSpec((tm, d), row), pl.BlockSpec((tm, d), row), mem_spec,
                  mem_spec, pl.BlockSpec((d, d), const), pl.BlockSpec((1, d), const), pl.BlockSpec((d, xa_w), const),
                  pl.BlockSpec((xa_w, d), const), pl.BlockSpec((1, d), const), pl.BlockSpec((d, nl), const),
                  pl.BlockSpec((d, nl), const), pl.BlockSpec((1, nl), const)],
        out_specs=[pl.BlockSpec((tm, d), row), pl.BlockSpec((tm, d), row), pl.BlockSpec((tm, nl), row)],
        out_shape=[jax.ShapeDtypeStruct((n, d), f32), jax.ShapeDtypeStruct((n, d), f32),
                   jax.ShapeDtypeStruct((n, nl), f32)],
        compiler_params=_params(("parallel",)), name="token_stage",
    )(cg, ag, x, mem_k, mem_v, w_out, g_x.reshape(1, d), w_xq, w_xo, g_moe.reshape(1, d), wr_hi, wr_lo, b_r)


def _router_body(lg_ref, ri_ref, rg_ref, cnt_ref, carry_ref, *, n_exp, n_grp):
    tm, nl = lg_ref.shape
    epg = n_exp // n_grp

    @pl.when(pl.program_id(0) == 0)
    def _():
        carry_ref[...] = jnp.zeros_like(carry_ref)

    lg = lg_ref[...]
    lane = lax.broadcasted_iota(i32, (tm, nl), 1)
    is_grp = (lane >= n_exp) & (lane < n_exp + n_grp)
    gl = jnp.where(is_grp, lg, -jnp.inf)
    gmax = jnp.max(gl, axis=-1, keepdims=True)
    g_top = jnp.min(jnp.where(gl == gmax, lane - n_exp, BIG_I), axis=-1, keepdims=True)
    g_prob = 1.0 / jnp.sum(jnp.where(is_grp, jnp.exp(lg - gmax), 0.0), axis=-1, keepdims=True)
    in_grp = (lane >= g_top * epg) & (lane < (g_top + 1) * epg)
    el = jnp.where(in_grp, lg, -jnp.inf)
    m1 = jnp.max(el, axis=-1, keepdims=True)
    e1 = jnp.min(jnp.where(el == m1, lane, BIG_I), axis=-1, keepdims=True)
    el2 = jnp.where(lane == e1, -jnp.inf, el)
    m2 = jnp.max(el2, axis=-1, keepdims=True)
    e2 = jnp.min(jnp.where(el2 == m2, lane, BIG_I), axis=-1, keepdims=True)
    r21 = jnp.exp(m2 - m1)
    p1 = 1.0 / (1.0 + r21)
    gate1 = g_prob * p1
    gate2 = g_prob * (r21 * p1)

    hit1 = lane == e1
    hit2 = lane == e2
    onehot = jnp.where(hit1 | hit2, 1.0, 0.0)
    rr = lax.broadcasted_iota(i32, (tm, tm), 0)
    cc = lax.broadcasted_iota(i32, (tm, tm), 1)
    lower = jnp.where(cc < rr, 1.0, 0.0).astype(bf16)
    before = jnp.dot(lower, onehot.astype(bf16), preferred_element_type=f32) + carry_ref[0:1, :]
    rank1 = jnp.sum(jnp.where(hit1, before, 0.0), axis=-1, keepdims=True).astype(i32)
    rank2 = jnp.sum(jnp.where(hit2, before, 0.0), axis=-1, keepdims=True).astype(i32)
    carry_ref[0:1, :] = carry_ref[0:1, :] + jnp.sum(onehot, axis=0, keepdims=True)

    ri = jnp.where(lane == 0, e1, jnp.where(lane == 1, e2, jnp.where(lane == 2, rank1, jnp.where(lane == 3, rank2, 0))))
    ri_ref[...] = ri
    rg_ref[...] = jnp.where(lane == 0, gate1, jnp.where(lane == 1, gate2, 0.0))
    cnt_ref[...] = carry_ref[...]


def _router(logits, n_exp, n_grp, tm):
    n, nl = logits.shape
    tm = min(tm, n)
    assert n % tm == 0
    return pl.pallas_call(
        functools.partial(_router_body, n_exp=n_exp, n_grp=n_grp),
        grid=(n // tm,),
        in_specs=[pl.BlockSpec((tm, nl), lambda i: (i, 0))],
        out_specs=[pl.BlockSpec((tm, nl), lambda i: (i, 0)), pl.BlockSpec((tm, nl), lambda i: (i, 0)),
                   pl.BlockSpec((8, nl), lambda i: (0, 0))],
        out_shape=[jax.ShapeDtypeStruct((n, nl), i32), jax.ShapeDtypeStruct((n, nl), f32),
                   jax.ShapeDtypeStruct((8, nl), f32)],
        scratch_shapes=[pltpu.VMEM((8, nl), f32)],
        compiler_params=_params(("arbitrary",)), name="router",
    )(logits)


def _dispatch_body(slot_ref, h_ref, zero_ref, xs_ref, sem, *, tg):
    del zero_ref
    i = pl.program_id(0)

    def issue(r, carry):
        tok = i * tg + r
        for kq in range(EXPERT_TOP_K):
            pltpu.make_async_copy(h_ref.at[tok], xs_ref.at[slot_ref[tok * EXPERT_TOP_K + kq]], sem).start()
        return carry

    lax.fori_loop(0, tg, issue, 0)

    def drain(r, carry):
        pltpu.make_async_copy(h_ref.at[0], xs_ref.at[0], sem).wait()
        return carry

    lax.fori_loop(0, tg * EXPERT_TOP_K, drain, 0)


def _dispatch(h3, slot, n_slots, tg):
    n, s, _ = h3.shape
    tg = min(tg, n)
    assert n % tg == 0
    zeros = jnp.zeros((n_slots, s, LANES), f32)
    return pl.pallas_call(
        functools.partial(_dispatch_body, tg=tg),
        grid_spec=pltpu.PrefetchScalarGridSpec(
            num_scalar_prefetch=1, grid=(n // tg,),
            in_specs=[pl.BlockSpec(memory_space=pl.ANY), pl.BlockSpec(memory_space=pl.ANY)],
            out_specs=pl.BlockSpec(memory_space=pl.ANY),
            scratch_shapes=[pltpu.SemaphoreType.DMA(())]),
        out_shape=jax.ShapeDtypeStruct((n_slots, s, LANES), f32),
        input_output_aliases={2: 0},
        compiler_params=_params(("arbitrary",)), name="dispatch",
    )(slot, h3, zeros)


def _ffn_body(be_ref, nu_ref, x_ref, w1_ref, w3_ref, w2_ref, y_ref):
    i = pl.program_id(0)

    @pl.when(i < nu_ref[0])
    def _():
        xb = x_ref[...].astype(bf16)
        a = jnp.dot(xb, w1_ref[0], preferred_element_type=f32)
        g = jnp.dot(xb, w3_ref[0], preferred_element_type=f32)
        mid = (a * _sigmoid(a) * g).astype(bf16)
        y_ref[...] = jnp.dot(mid, w2_ref[0], preferred_element_type=f32)

    @pl.when(i >= nu_ref[0])
    def _():
        y_ref[...] = jnp.zeros_like(y_ref)


def _expert_ffn(xs, blk_expert, n_used, w1, w3, w2, rb):
    ns, d = xs.shape
    n_exp, _, de = w1.shape
    return pl.pallas_call(
        _ffn_body,
        grid_spec=pltpu.PrefetchScalarGridSpec(
            num_scalar_prefetch=2, grid=(ns // rb,),
            in_specs=[pl.BlockSpec((rb, d), lambda i, be, nu: (i, 0)),
                      pl.BlockSpec((1, d, de), lambda i, be, nu: (be[i], 0, 0)),
                      pl.BlockSpec((1, d, de), lambda i, be, nu: (be[i], 0, 0)),
                      pl.BlockSpec((1, de, d), lambda i, be, nu: (be[i], 0, 0))],
            out_specs=pl.BlockSpec((rb, d), lambda i, be, nu: (i, 0))),
        out_shape=jax.ShapeDtypeStruct((ns, d), f32),
        compiler_params=_params(("arbitrary",)), name="expert_ffn",
    )(blk_expert, n_used, xs, w1, w3, w2)


def _combine_body(slot_ref, x2_ref, rg_ref, gf_ref, ys_ref, o_ref, ybuf, sem, *, tc, s):
    i = pl.program_id(0)

    def issue(r, carry):
        tok = i * tc + r
        for kq in range(EXPERT_TOP_K):
            pltpu.make_async_copy(ys_ref.at[slot_ref[tok * EXPERT_TOP_K + kq]],
                                  ybuf.at[kq, pl.ds(pl.multiple_of(r * s, s), s), :], sem).start()
        return carry

    lax.fori_loop(0, tc, issue, 0)

    def drain(r, carry):
        pltpu.make_async_copy(ys_ref.at[0], ybuf.at[0, pl.ds(0, s), :], sem).wait()
        return carry

    lax.fori_loop(0, tc * EXPERT_TOP_K, drain, 0)

    rg = rg_ref[...]
    lane = lax.broadcasted_iota(i32, rg.shape, 1)
    gates = [jnp.sum(jnp.where(lane == kq, rg, 0.0), axis=-1, keepdims=True) for kq in range(EXPERT_TOP_K)]
    ss = jnp.zeros((tc, 1), f32)
    for c in range(s):
        cs = slice(c * LANES, (c + 1) * LANES)
        val = x2_ref[:, cs]
        for kq in range(EXPERT_TOP_K):
            val = val + gates[kq] * ybuf[kq, pl.ds(c, tc, stride=s), :]
        o_ref[:, cs] = val
        ss = ss + jnp.sum(val * val, axis=-1, keepdims=True)
    o_ref[...] = o_ref[...] * lax.rsqrt(ss * (1.0 / (s * LANES)) + RMS_EPS) * gf_ref[...]


def _combine(x2, gates, g_final, ys3, slot, tc):
    n, d = x2.shape
    s = d // LANES
    tc = min(tc, n)
    assert n % tc == 0
    return pl.pallas_call(
        functools.partial(_combine_body, tc=tc, s=s),
        grid_spec=pltpu.PrefetchScalarGridSpec(
            num_scalar_prefetch=1, grid=(n // tc,),
            in_specs=[pl.BlockSpec((tc, d), lambda i, sl: (i, 0)), pl.BlockSpec((tc, LANES), lambda i, sl: (i, 0)),
                      pl.BlockSpec((1, d), lambda i, sl: (0, 0)), pl.BlockSpec(memory_space=pl.ANY)],
            out_specs=pl.BlockSpec((tc, d), lambda i, sl: (i, 0)),
            scratch_shapes=[pltpu.VMEM((EXPERT_TOP_K, tc * s, LANES), f32), pltpu.SemaphoreType.DMA(())]),
        out_shape=jax.ShapeDtypeStruct((n, d), f32),
        compiler_params=_params(("arbitrary",)), name="combine",
    )(slot, x2, gates, g_final.reshape(1, d), ys3)


def _moe_and_final(x2, h2, logits, g_final, w1, w3, w2, n_grp, *, tm_route, rb, tg, tc):
    n, d = x2.shape
    n_exp = w1.shape[0]
    s = d // LANES
    ri, rg, cnt = _router(logits, n_exp, n_grp, tm_route)
    sizes = cnt[0, :n_exp].astype(i32)
    padded = (sizes + rb - 1) // rb * rb
    pad_end = jnp.cumsum(padded)
    pad_start = pad_end - padded
    slot = (pad_start[ri[:, :EXPERT_TOP_K]] + ri[:, EXPERT_TOP_K:2 * EXPERT_TOP_K]).reshape(-1).astype(i32)
    n_blocks = -(-(n * EXPERT_TOP_K) // rb) + n_exp
    blk_expert = jnp.minimum(jnp.searchsorted(pad_end, jnp.arange(n_blocks, dtype=i32) * rb, side="right"),
                             n_exp - 1).astype(i32)
    n_used = (pad_end[-1:] // rb).astype(i32)
    xs = _dispatch(h2.reshape(n, s, LANES), slot, n_blocks * rb, tg)
    ys = _expert_ffn(xs.reshape(n_blocks * rb, d), blk_expert, n_used, w1, w3, w2, rb)
    return _combine(x2, rg, g_final, ys.reshape(n_blocks * rb, s, LANES), slot, tc)


def _layer(l, xp, xs, cache_k, cache_v, cache_conv, cache_mem_k, cache_mem_v, page_table, mem_prompt, rel_table,
           g_mix, w_in, w_dw, b_dw, ln_g, ln_b, w_pw, w_out, g_mem, w_xk, w_xv, g_xattn, w_xq, w_xo, g_moe,
           w_rg, b_rg, w_re, b_re, w1, w3, w2, g_final, is_last):
    assert is_last, "the final norm is fused into the last layer's combine kernel"
    bp, s_len, d = xp.shape
    bs, t_len, _ = xs.shape
    page, n_heads, dh = cache_k.shape[2:]
    width, dc = w_dw.shape[1:]
    attn_w = n_heads * dh
    n_mem, xa_heads, xa_dh = cache_mem_k.shape[2:]
    xa_w = xa_heads * xa_dh
    n_grp, n_exp = w_rg.shape[-1], w_re.shape[-1]

    w_in_b = w_in[l].astype(bf16)
    w_pw_b, w_out_b = w_pw[l].astype(bf16), w_out[l].astype(bf16)
    w_xq_b, w_xo_b = w_xq[l].astype(bf16), w_xo[l].astype(bf16)
    w_mem_b = jnp.concatenate([w_xk[l], w_xv[l]], axis=1).astype(bf16)
    w1_b, w3_b, w2_b = w1[l].astype(bf16), w3[l].astype(bf16), w2[l].astype(bf16)
    nl = -(-(n_exp + n_grp) // LANES) * LANES
    w_r = jnp.pad(jnp.concatenate([w_re[l], w_rg[l]], axis=1), ((0, 0), (0, nl - n_exp - n_grp)))
    wr_hi = w_r.astype(bf16)
    wr_lo = (w_r - wr_hi.astype(f32)).astype(bf16)
    b_r = jnp.pad(jnp.concatenate([b_re[l], b_rg[l]]), (0, nl - n_exp - n_grp)).reshape(1, nl)
    pieces = [(0, 2 * dc), (2 * dc, attn_w), (2 * dc + attn_w, attn_w), (2 * dc + 2 * attn_w, attn_w),
              (2 * dc + 3 * attn_w, d), (2 * dc + 3 * attn_w + d, d)]

    n_p = bp * s_len
    glu, q, k, v, gc, ga = _norm_proj(xp.reshape(n_p, d), g_mix[l], w_in_b, pieces, tm=512)
    cg, conv_p = _conv_branch(glu.reshape(bp, s_len, 2 * dc), gc.reshape(bp, s_len, d), None, w_dw[l], b_dw[l],
                              ln_g[l], ln_b[l], w_pw_b, ts=256)
    ag = _moba_prompt(q.reshape(bp, s_len, attn_w), k.reshape(bp, s_len, attn_w), v.reshape(bp, s_len, attn_w),
                      ga.reshape(bp, s_len, attn_w), rel_table, page)
    mk, mv = _norm_proj(mem_prompt.reshape(bp * n_mem, d), g_mem[l], w_mem_b, [(0, xa_w), (xa_w, xa_w)], tm=512)
    x2, h2, logits = _token_stage(cg.reshape(n_p, d), ag.reshape(n_p, d), xp.reshape(n_p, d),
                                  mk.reshape(bp, n_mem, xa_w), mv.reshape(bp, n_mem, xa_w), w_out_b, g_xattn[l],
                                  w_xq_b, w_xo_b, g_moe[l], wr_hi, wr_lo, b_r, tm=256, rows_per_seq=s_len,
                                  xa_heads=xa_heads)
    y_p = _moe_and_final(x2, h2, logits, g_final, w1_b, w3_b, w2_b, n_grp, tm_route=256, rb=256, tg=256, tc=128)
    prompt_out = (y_p.reshape(bp, s_len, d), k.reshape(bp, s_len, n_heads, dh), v.reshape(bp, s_len, n_heads, dh),
                  conv_p, mk.reshape(bp, n_mem, xa_heads, xa_dh), mv.reshape(bp, n_mem, xa_heads, xa_dh))

    rows = SAMPLE_ROWS
    assert t_len <= rows
    n_s = bs * rows
    xs_pad = jnp.pad(xs, ((0, 0), (0, rows - t_len), (0, 0))).reshape(n_s, d)
    glu, q, k, v, gc, ga = _norm_proj(xs_pad, g_mix[l], w_in_b, pieces, tm=512)
    state = jnp.pad(cache_conv[l], ((0, 0), (HALO - (width - 1), 0), (0, 0)))
    cg, conv_s = _conv_branch(glu.reshape(bs, rows, 2 * dc), gc.reshape(bs, rows, d), state, w_dw[l], b_dw[l],
                              ln_g[l], ln_b[l], w_pw_b, ts=rows, t_real=t_len)
    ag = _moba_sample(q.reshape(bs, rows, attn_w), k.reshape(bs, rows, attn_w), v.reshape(bs, rows, attn_w),
                      ga.reshape(bs, rows, attn_w), cache_k, cache_v, l, page_table, rel_table, t_len)
    x2, h2, logits = _token_stage(cg.reshape(n_s, d), ag.reshape(n_s, d), xs_pad,
                                  cache_mem_k[l].reshape(bs, n_mem, xa_w), cache_mem_v[l].reshape(bs, n_mem, xa_w),
                                  w_out_b, g_xattn[l], w_xq_b, w_xo_b, g_moe[l], wr_hi, wr_lo, b_r, tm=rows,
                                  rows_per_seq=rows, xa_heads=xa_heads)
    real = lambda a: a.reshape(bs, rows, -1)[:, :t_len].reshape(bs * t_len, -1)
    y_s = _moe_and_final(real(x2), real(h2), real(logits), g_final, w1_b, w3_b, w2_b, n_grp, tm_route=128, rb=16,
                         tg=128, tc=128)
    sample_out = (y_s.reshape(bs, t_len, d), real(k).reshape(bs, t_len, n_heads, dh),
                  real(v).reshape(bs, t_len, n_heads, dh), conv_s)
    return prompt_out, sample_out


def kernel(x_prompt, x_sample, cache_k, cache_v, cache_conv, cache_mem_k, cache_mem_v, page_table, mem_prompt, rel_table, g_mix, w_in, w_dw, b_dw, ln_g, ln_b, w_pw, w_out, g_mem, w_xk, w_xv, g_xattn, w_xq, w_xo, g_moe, w_rg, b_rg, w_re, b_re, w1, w3, w2, g_final):
    depth = w_in.shape[0]
    assert depth == 1, "one layer per step"
    (y_p, k_p, v_p, conv_p, mk_p, mv_p), (y_s, k_s, v_s, conv_s) = _layer(
        0, x_prompt, x_sample, cache_k, cache_v, cache_conv, cache_mem_k, cache_mem_v, page_table, mem_prompt,
        rel_table, g_mix, w_in, w_dw, b_dw, ln_g, ln_b, w_pw, w_out, g_mem, w_xk, w_xv, g_xattn, w_xq, w_xo, g_moe,
        w_rg, b_rg, w_re, b_re, w1, w3, w2, g_final, is_last=True)
    return (y_p, y_s, k_p[None], v_p[None], conv_p[None], mk_p[None], mv_p[None], k_s[None], v_s[None],
            conv_s[None])
```

```python
import functools
import math

import numpy as np
import jax
import jax.numpy as jnp
from jax import lax
from jax.experimental import pallas as pl
from jax.experimental.pallas import tpu as pltpu

f32 = jnp.float32
bf16 = jnp.bfloat16
i32 = jnp.int32

MOBA_BLOCK = 256
MOBA_TOP_K = 3
MAX_DISTANCE = 128
EXPERT_TOP_K = 2
RMS_EPS = 1e-6
LN_EPS = 1e-5

LANES = 128
SUBLANES = 8
SAMPLE_ROWS = 16
MASKED = -1e30
BIG_I = 1 << 20
LOG2E = 1.4426950408889634
VMEM_LIMIT = 56 * 1024 * 1024


def _params(sem, vmem=VMEM_LIMIT):
    return pltpu.CompilerParams(dimension_semantics=sem, vmem_limit_bytes=vmem)


def _sigmoid(x):
    return 1.0 / (1.0 + jnp.exp(-x))


def _rms(x, g, eps=RMS_EPS):
    return x * lax.rsqrt(jnp.mean(x * x, axis=-1, keepdims=True) + eps) * g


def _nt(a, b):
    return lax.dot_general(a, b, (((1,), (1,)), ((), ())), preferred_element_type=f32)


def _round_bf16(x):
    return x.astype(bf16).astype(f32)


def _t5_bucket_np(dist, n_buckets):
    dist = np.maximum(dist, 0)
    max_exact = n_buckets // 2
    ratio = (np.log(np.maximum(dist, 1).astype(np.float32) / np.float32(max_exact))
             / np.float32(math.log(MAX_DISTANCE / max_exact))).astype(np.float32)
    large = np.minimum(max_exact + (ratio * np.float32(n_buckets - max_exact)).astype(np.int32), n_buckets - 1)
    return np.where(dist < max_exact, dist, large).astype(np.int32)


def _norm_proj_body(x_ref, g_ref, *refs, n_out):
    w_refs, o_refs, hn_ref = refs[:n_out], refs[n_out:2 * n_out], refs[2 * n_out]

    @pl.when(pl.program_id(1) == 0)
    def _():
        hn_ref[...] = _rms(x_ref[...], g_ref[...]).astype(bf16)

    h = hn_ref[...]
    for w_ref, o_ref in zip(w_refs, o_refs):
        o_ref[...] = jnp.dot(h, w_ref[...], preferred_element_type=f32).astype(o_ref.dtype)


def _norm_proj(x, g, w, pieces, tm):
    m, k = x.shape
    tm = min(tm, m)
    assert m % tm == 0
    min_w = min(wd for _, wd in pieces)
    nj = max(1, min_w // 256)
    in_specs = [pl.BlockSpec((tm, k), lambda i, j: (i, 0)), pl.BlockSpec((1, k), lambda i, j: (0, 0))]
    out_specs, out_shape = [], []
    for off, wd in pieces:
        assert wd % nj == 0 and off % (wd // nj) == 0
        tn = wd // nj
        in_specs.append(pl.BlockSpec((k, tn), functools.partial(lambda i, j, o: (0, o + j), o=off // tn)))
        out_specs.append(pl.BlockSpec((tm, tn), lambda i, j: (i, j)))
        out_shape.append(jax.ShapeDtypeStruct((m, wd), f32))
    return pl.pallas_call(
        functools.partial(_norm_proj_body, n_out=len(pieces)),
        grid=(m // tm, nj), in_specs=in_specs, out_specs=out_specs, out_shape=out_shape,
        scratch_shapes=[pltpu.VMEM((tm, k), bf16)],
        compiler_params=_params(("parallel", "arbitrary")), name="norm_proj",
    )(x, g.reshape(1, k), *([w] * len(pieces)))


HALO = 32


def _conv_body(glu_ref, halo_ref, gc_ref, wdw_ref, bdw_ref, lng_ref, lnb_ref, wpw_ref, cg_ref, st_ref,
               hist_ref, histr_ref, y_ref, *, dc, width, ts, t_real, halo_is_state, n_tiles):
    i = pl.program_id(1)
    glu = glu_ref[0]
    hist_ref[HALO:HALO + ts, :] = glu[:, :dc] * _sigmoid(glu[:, dc:])
    if halo_is_state:
        hist_ref[0:HALO, :] = halo_ref[0]
    else:
        hl = halo_ref[0]
        uh = hl[:, :dc] * _sigmoid(hl[:, dc:])
        hist_ref[0:HALO, :] = jnp.where(i == 0, 0.0, uh)
    histr_ref[0:HALO + ts, :] = _round_bf16(hist_ref[0:HALO + ts, :])
    histr_ref[HALO + ts:HALO + ts + SUBLANES, :] = jnp.zeros((SUBLANES, dc), f32)
    base = HALO - (width - 1)
    rs = min(ts, 128)
    for c in range(dc // LANES):
        cs = slice(c * LANES, (c + 1) * LANES)
        for r0 in range(0, ts, rs):
            acc = bdw_ref[:, cs]
            for r in range(SUBLANES):
                part = None
                for w in range(width):
                    if (base + w) % SUBLANES == r:
                        lo = r0 + base + w - r
                        term = histr_ref[lo:lo + rs + SUBLANES, cs] * wdw_ref[w:w + 1, cs]
                        part = term if part is None else part + term
                if part is not None:
                    acc = acc + part[r:r + rs]
            y_ref[r0:r0 + rs, cs] = acc
    y = y_ref[...]
    yc = y - jnp.mean(y, axis=-1, keepdims=True)
    var = jnp.mean(yc * yc, axis=-1, keepdims=True)
    z = yc * lax.rsqrt(var + LN_EPS) * lng_ref[...] + lnb_ref[...]
    z = z * _sigmoid(z)
    conv_out = jnp.dot(z.astype(bf16), wpw_ref[...], preferred_element_type=f32)
    cg_ref[0] = (_sigmoid(gc_ref[0]) * conv_out).astype(cg_ref.dtype)

    @pl.when(i == n_tiles - 1)
    def _():
        last = HALO + (ts if t_real is None else t_real)
        st_ref[0] = hist_ref[last - (width - 1):last, :]


def _conv_branch(glu, gc, state, w_dw, b_dw, ln_g, ln_b, w_pw, *, ts, t_real=None, out_dtype=bf16):
    b, t, _ = glu.shape
    width, dc = w_dw.shape
    d = w_pw.shape[1]
    ts = min(ts, t)
    assert t % ts == 0 and ts % HALO == 0 or t == ts
    assert width - 1 <= HALO and dc % LANES == 0
    n_tiles = t // ts
    halo_is_state = state is not None
    if halo_is_state:
        assert n_tiles == 1
        halo_arr = state
        halo_spec = pl.BlockSpec((1, HALO, dc), lambda bi, i: (bi, 0, 0))
    else:
        assert t >= width - 1
        halo_arr = glu
        r = ts // HALO
        halo_spec = pl.BlockSpec((1, HALO, 2 * dc), lambda bi, i: (bi, jnp.maximum(i * r - 1, 0), 0))
    const = lambda bi, i: (0, 0)
    return pl.pallas_call(
        functools.partial(_conv_body, dc=dc, width=width, ts=ts, t_real=t_real, halo_is_state=halo_is_state,
                          n_tiles=n_tiles),
        grid=(b, n_tiles),
        in_specs=[pl.BlockSpec((1, ts, 2 * dc), lambda bi, i: (bi, i, 0)), halo_spec,
                  pl.BlockSpec((1, ts, d), lambda bi, i: (bi, i, 0)),
                  pl.BlockSpec((width, dc), const), pl.BlockSpec((1, dc), const), pl.BlockSpec((1, dc), const),
                  pl.BlockSpec((1, dc), const), pl.BlockSpec((dc, d), const)],
        out_specs=[pl.BlockSpec((1, ts, d), lambda bi, i: (bi, i, 0)),
                   pl.BlockSpec((1, width - 1, dc), lambda bi, i: (bi, 0, 0))],
        out_shape=[jax.ShapeDtypeStruct((b, t, d), out_dtype), jax.ShapeDtypeStruct((b, width - 1, dc), f32)],
        scratch_shapes=[pltpu.VMEM((HALO + ts, dc), f32), pltpu.VMEM((HALO + ts + SUBLANES, dc), f32),
                        pltpu.VMEM((ts, dc), f32)],
        compiler_params=_params(("parallel", "arbitrary")), name="conv_branch",
    )(glu, halo_arr, gc, w_dw, b_dw.reshape(1, dc), ln_g.reshape(1, dc), ln_b.reshape(1, dc), w_pw)


def _bias_body(tab_ref, map_ref, o_ref, *, n_buckets):
    h = pl.program_id(0)
    bm = map_ref[...]
    acc = jnp.zeros(bm.shape, f32)
    for bk in range(n_buckets):
        acc = jnp.where(bm == bk, tab_ref[bk, h], acc)
    o_ref[0] = acc


def _bias_tiles(rel_table, bucket_map):
    n_buckets, n_heads = rel_table.shape
    r, c = bucket_map.shape
    return pl.pallas_call(
        functools.partial(_bias_body, n_buckets=n_buckets),
        grid=(n_heads,),
        in_specs=[pl.BlockSpec(memory_space=pltpu.SMEM), pl.BlockSpec((r, c), lambda h: (0, 0))],
        out_specs=pl.BlockSpec((1, r, c), lambda h: (h, 0, 0)),
        out_shape=jax.ShapeDtypeStruct((n_heads, r, c), f32),
        compiler_params=_params(("parallel",)), name="bias_tiles",
    )(rel_table, jnp.asarray(bucket_map))


def _moba_prompt_body(tab_ref, q_ref, k_ref, v_ref, ga_ref, bias_ref, o_ref, vt_ref, kb_ref, sel_ref, s_ref, *,
                      s_len, dh, page, n_buckets):
    blk = MOBA_BLOCK
    nb = s_len // blk
    ppb = blk // page
    h = pl.program_id(1)
    c2 = dh ** -0.5 * LOG2E
    far2 = tab_ref[n_buckets - 1, h] * LOG2E

    rows = []
    for n in range(nb):
        acc = jnp.zeros((1, dh), f32)
        for p in range(ppb):
            acc = acc + jnp.sum(k_ref[0, n * blk + p * page:n * blk + (p + 1) * page, :], axis=0,
                                keepdims=True) * (1.0 / page)
        rows.append(acc * (1.0 / ppb))
        vt_ref[n] = jnp.transpose(v_ref[0, n * blk:(n + 1) * blk, :]).astype(bf16)
        kb_ref[n] = k_ref[0, n * blk:(n + 1) * blk, :].astype(bf16)
    bmean = jnp.concatenate(rows, axis=0)
    sc = _nt(_round_bf16(bmean), _round_bf16(q_ref[0]))
    n_idx = lax.broadcasted_iota(i32, (nb, s_len), 0)
    own = lax.broadcasted_iota(i32, (nb, s_len), 1) // blk
    valid = n_idx < own
    scm = jnp.where(valid, sc, -jnp.inf)
    rank = jnp.zeros((nb, s_len), f32)
    for m in range(nb - 1):
        row = scm[m:m + 1, :]
        beats = (row > scm) | ((row == scm) & (m < n_idx))
        rank = rank + jnp.where(beats, 1.0, 0.0)
    sel_add = jnp.where(valid & (rank < MOBA_TOP_K), 0.0, MASKED)
    for j in range(nb):
        sel_ref[j] = sel_add[:, j * blk:(j + 1) * blk]

    kk = lax.broadcasted_iota(i32, (blk, blk), 0)
    qq = lax.broadcasted_iota(i32, (blk, blk), 1)
    own_b = jnp.where(kk <= qq, bias_ref[0, 0:blk, :] * LOG2E, MASKED)
    adj_b = bias_ref[0, blk:2 * blk, :] * LOG2E

    for j in range(nb):
        qj = q_ref[0, j * blk:(j + 1) * blk, :].astype(bf16)
        blocks = [j] + ([j - 1] if j >= 1 else []) + list(range(j - 1))
        m = None
        for t, n in enumerate(blocks):
            if n == j:
                add = own_b
            elif n == j - 1:
                add = adj_b + sel_ref[j, n:n + 1, :]
            else:
                add = far2 + sel_ref[j, n:n + 1, :]
            s = _nt(kb_ref[n], qj) * c2 + add
            s_ref[t] = s
            mt = jnp.max(s, axis=0, keepdims=True)
            m = mt if m is None else jnp.maximum(m, mt)
        l = jnp.zeros((1, blk), f32)
        acc = jnp.zeros((dh, blk), f32)
        for t, n in enumerate(blocks):
            p = jnp.exp2(s_ref[t] - m)
            l = l + jnp.sum(p, axis=0, keepdims=True)
            acc = acc + jnp.dot(vt_ref[n], p.astype(bf16), preferred_element_type=f32)
        out = jnp.transpose(acc * (1.0 / l))
        gate = _sigmoid(ga_ref[0, j * blk:(j + 1) * blk, :])
        o_ref[0, j * blk:(j + 1) * blk, :] = (gate * out).astype(o_ref.dtype)


def _moba_prompt(q, k, v, ga, rel_table, page):
    b, s_len, width = q.shape
    n_buckets, n_heads = rel_table.shape
    dh = width // n_heads
    blk = MOBA_BLOCK
    assert s_len % blk == 0 and blk % page == 0 and dh % LANES == 0
    assert blk + 1 >= MAX_DISTANCE, "blocks two or more away must share the last bucket"
    kk = np.arange(blk)[:, None]
    qq = np.arange(blk)[None, :]
    bmap = np.concatenate([_t5_bucket_np(qq - kk, n_buckets), _t5_bucket_np(qq - kk + blk, n_buckets)], axis=0)
    bias = _bias_tiles(rel_table, bmap)
    nb = s_len // blk
    qkv_spec = pl.BlockSpec((1, s_len, dh), lambda bi, hi: (bi, 0, hi))
    return pl.pallas_call(
        functools.partial(_moba_prompt_body, s_len=s_len, dh=dh, page=page, n_buckets=n_buckets),
        grid=(b, n_heads),
        in_specs=[pl.BlockSpec(memory_space=pltpu.SMEM), qkv_spec, qkv_spec, qkv_spec, qkv_spec,
                  pl.BlockSpec((1, 2 * blk, blk), lambda bi, hi: (hi, 0, 0))],
        out_specs=pl.BlockSpec((1, s_len, dh), lambda bi, hi: (bi, 0, hi)),
        out_shape=jax.ShapeDtypeStruct((b, s_len, width), bf16),
        scratch_shapes=[pltpu.VMEM((nb, dh, blk), bf16), pltpu.VMEM((nb, blk, dh), bf16),
                        pltpu.VMEM((nb, nb, blk), f32), pltpu.VMEM((nb, blk, blk), f32)],
        compiler_params=_params(("parallel", "parallel")), name="moba_prompt",
    )(rel_table, q, k, v, ga, bias)


def _page_means_body(pt_ref, *refs, n_in, page, ppb):
    page_refs, o_ref = refs[:n_in], refs[n_in]
    for blk_i in range(n_in // ppb):
        acc = jnp.zeros(o_ref.shape[2:], f32)
        for p in range(ppb):
            acc = acc + jnp.sum(page_refs[blk_i * ppb + p][...], axis=0) * (1.0 / page)
        o_ref[0, blk_i] = acc * (1.0 / ppb)


def _block_means(cache_k, layer, page_table, n_blocks):
    _, _, page, n_heads, dh = cache_k.shape
    bs, n_pages = page_table.shape
    ppb = MOBA_BLOCK // page
    bps = math.gcd(n_blocks, 4)
    n_in = bps * ppb

    def page_spec(p):
        return pl.BlockSpec((None, None, page, n_heads, dh),
                            lambda b, c, pt: (layer, pt[b * n_pages + c * n_in + p], 0, 0, 0))

    return pl.pallas_call(
        functools.partial(_page_means_body, n_in=n_in, page=page, ppb=ppb),
        grid_spec=pltpu.PrefetchScalarGridSpec(
            num_scalar_prefetch=1, grid=(bs, n_blocks // bps),
            in_specs=[page_spec(p) for p in range(n_in)],
            out_specs=pl.BlockSpec((1, bps, n_heads, dh), lambda b, c, pt: (b, c, 0, 0))),
        out_shape=jax.ShapeDtypeStruct((bs, n_blocks, n_heads, dh), f32),
        compiler_params=_params(("parallel", "parallel")), name="block_means",
    )(page_table.reshape(-1), *([cache_k] * n_in))


def _select_body(q_ref, bm_ref, o_ref, *, n_top, n_heads, dh):
    out_lane = lax.broadcasted_iota(i32, o_ref.shape[2:], 1)
    for h in range(n_heads):
        cs = slice(h * dh, (h + 1) * dh)
        sc = _nt(_round_bf16(q_ref[0, :, cs]), _round_bf16(bm_ref[0, :, cs]))
        lane = lax.broadcasted_iota(i32, sc.shape, 1)
        out = jnp.zeros(o_ref.shape[2:], i32)
        for kq in range(n_top):
            best = jnp.max(sc, axis=-1, keepdims=True)
            idx = jnp.min(jnp.where(sc == best, lane, BIG_I), axis=-1, keepdims=True)
            out = jnp.where(out_lane == kq, idx, out)
            sc = jnp.where(lane == idx, -jnp.inf, sc)
        o_ref[0, h] = out


def _select_blocks(q, bmeans, n_heads, n_top):
    bs, rows, width = q.shape
    dh = width // n_heads
    n_blocks = bmeans.shape[1]
    return pl.pallas_call(
        functools.partial(_select_body, n_top=n_top, n_heads=n_heads, dh=dh),
        grid=(bs,),
        in_specs=[pl.BlockSpec((1, rows, width), lambda b: (b, 0, 0)),
                  pl.BlockSpec((1, n_blocks, width), lambda b: (b, 0, 0))],
        out_specs=pl.BlockSpec((1, n_heads, rows, LANES), lambda b: (b, 0, 0, 0)),
        out_shape=jax.ShapeDtypeStruct((bs, n_heads, rows, LANES), i32),
        compiler_params=_params(("parallel",)), name="select_blocks",
    )(q, bmeans)


def _moba_sample_body(phys_ref, sel_ref, tab_ref, ck_ref, cv_ref, q_ref, kn_ref, vn_ref, ga_ref, adj_ref, own_ref,
                      o_ref, kbuf, vbuf, sem, *, layer, t_len, n_top, ppb, page, n_heads, n_full, n_buckets, dh,
                      n_steps):
    b, h = pl.program_id(0), pl.program_id(1)
    blk = ppb * page
    scale = dh ** -0.5
    far_bias = tab_ref[n_buckets - 1, h]
    n_pg = n_top * ppb
    step = b * n_heads + h
    slot = step % 2

    def copies(cb, ch, sl, t, c):
        pg = phys_ref[((cb * t_len + t) * n_heads + ch) * n_pg + c]
        dst = pl.ds(c * page, page)
        return (pltpu.make_async_copy(ck_ref.at[layer, pg, :, ch, :], kbuf.at[sl, t, dst, :], sem.at[sl, 0]),
                pltpu.make_async_copy(cv_ref.at[layer, pg, :, ch, :], vbuf.at[sl, t, dst, :], sem.at[sl, 1]))

    def start_all(cb, ch, sl):
        for t in range(t_len):
            for c in range(n_pg):
                ck, cv = copies(cb, ch, sl, t, c)
                ck.start()
                cv.start()

    @pl.when(step == 0)
    def _():
        start_all(b, h, slot)

    @pl.when(step + 1 < n_steps)
    def _():
        nxt = step + 1
        start_all(nxt // n_heads, nxt % n_heads, 1 - slot)

    for t in range(t_len):
        for c in range(n_pg):
            ck, cv = copies(b, h, slot, t, c)
            ck.wait()
            cv.wait()

    rows = q_ref.shape[1]
    row_id = lax.broadcasted_iota(i32, (rows, 1), 0)
    out = jnp.zeros((rows, dh), f32)
    kn = _round_bf16(kn_ref[0])
    vn = _round_bf16(vn_ref[0])
    for t in range(t_len):
        qt = _round_bf16(q_ref[0, t:t + 1, :])
        s_sel = jnp.sum(_round_bf16(kbuf[slot, t]) * qt, axis=-1, keepdims=True) * scale
        biases = []
        for c in range(n_top):
            sblk = sel_ref[((b * t_len + t) * n_heads + h) * n_top + c]
            biases.append(jnp.where(sblk == n_full - 1, adj_ref[0, t * blk:(t + 1) * blk, :], far_bias))
        s_sel = s_sel + jnp.concatenate(biases, axis=0)
        s_own = jnp.sum(kn * qt, axis=-1, keepdims=True) * scale + own_ref[0, t * rows:(t + 1) * rows, :]
        s_own = jnp.where(row_id <= t, s_own, MASKED)
        m = jnp.maximum(jnp.max(s_sel, axis=0, keepdims=True), jnp.max(s_own, axis=0, keepdims=True))
        p_sel = jnp.exp(s_sel - m)
        p_own = jnp.exp(s_own - m)
        l = jnp.sum(p_sel, axis=0, keepdims=True) + jnp.sum(p_own, axis=0, keepdims=True)
        p_sel = _round_bf16(p_sel / l)
        p_own = _round_bf16(p_own / l)
        o_t = (jnp.sum(p_sel * _round_bf16(vbuf[slot, t]), axis=0, keepdims=True)
               + jnp.sum(p_own * vn, axis=0, keepdims=True))
        out = jnp.where(row_id == t, o_t, out)
    o_ref[0] = (_sigmoid(ga_ref[0]) * out).astype(o_ref.dtype)


def _moba_sample(q, k_new, v_new, ga, cache_k, cache_v, layer, page_table, rel_table, t_len):
    bs, rows, width = q.shape
    _, _, page, n_heads, dh = cache_k.shape
    n_buckets = rel_table.shape[0]
    n_pages = page_table.shape[1]
    blk = MOBA_BLOCK
    ppb = blk // page
    past = n_pages * page
    n_full = past // blk
    assert past % blk == 0, "the new tokens must start a fresh MoBA block"
    assert t_len <= rows and t_len <= blk // 2 and blk + 1 >= MAX_DISTANCE
    n_top = min(MOBA_TOP_K, n_full)
    assert n_top >= 1
    bmeans = _block_means(cache_k, layer, page_table, n_full)
    sel = _select_blocks(q, bmeans.reshape(bs, n_full, width), n_heads, n_top)
    sel = jnp.transpose(sel[:, :, :t_len, :n_top], (0, 2, 1, 3))
    logical = sel[..., None] * ppb + jnp.arange(ppb, dtype=i32)
    phys = jnp.take_along_axis(page_table[:, None, None, :], logical.reshape(bs, t_len, n_heads, n_top * ppb),
                               axis=-1)
    tt = np.arange(8)[:, None]
    adj_map = _t5_bucket_np(blk + tt - np.arange(blk)[None, :], n_buckets)
    own_map = _t5_bucket_np(tt - np.arange(LANES)[None, :], n_buckets)
    tiles = _bias_tiles(rel_table, np.concatenate([adj_map, own_map], axis=1))
    adj = tiles[:, :t_len, :blk].reshape(n_heads, t_len * blk, 1)
    own = tiles[:, :t_len, blk:blk + rows].reshape(n_heads, t_len * rows, 1)
    tok_spec = pl.BlockSpec((1, rows, dh), lambda b, h, *_: (b, 0, h))
    return pl.pallas_call(
        functools.partial(_moba_sample_body, layer=layer, t_len=t_len, n_top=n_top, ppb=ppb, page=page,
                          n_heads=n_heads, n_full=n_full, n_buckets=n_buckets, dh=dh, n_steps=bs * n_heads),
        grid_spec=pltpu.PrefetchScalarGridSpec(
            num_scalar_prefetch=2, grid=(bs, n_heads),
            in_specs=[pl.BlockSpec(memory_space=pltpu.SMEM), pl.BlockSpec(memory_space=pl.ANY),
                      pl.BlockSpec(memory_space=pl.ANY), tok_spec, tok_spec, tok_spec, tok_spec,
                      pl.BlockSpec((1, t_len * blk, 1), lambda b, h, *_: (h, 0, 0)),
                      pl.BlockSpec((1, t_len * rows, 1), lambda b, h, *_: (h, 0, 0))],
            out_specs=pl.BlockSpec((1, rows, dh), lambda b, h, *_: (b, 0, h)),
            scratch_shapes=[pltpu.VMEM((2, t_len, n_top * blk, dh), f32),
                            pltpu.VMEM((2, t_len, n_top * blk, dh), f32), pltpu.SemaphoreType.DMA((2, 2))]),
        out_shape=jax.ShapeDtypeStruct((bs, rows, width), f32),
        compiler_params=_params(("arbitrary", "arbitrary")), name="moba_sample",
    )(phys.reshape(-1).astype(i32), sel.reshape(-1).astype(i32), rel_table, cache_k, cache_v, q, k_new, v_new, ga,
      adj, own)


def _token_body(cg_ref, ag_ref, x_ref, mk_ref, mv_ref, wout_ref, gx_ref, wxq_ref, wxo_ref, gm_ref, wr_ref, br_ref,
                x2_ref, h2_ref, lg_ref, *, xa_heads, xa_dh):
    merged = (cg_ref[...].astype(f32) + ag_ref[...].astype(f32)).astype(bf16)
    x1 = x_ref[...] + jnp.dot(merged, wout_ref[...], preferred_element_type=f32)
    hx = _rms(x1, gx_ref[...]).astype(bf16)
    qx = jnp.dot(hx, wxq_ref[...], preferred_element_type=f32)
    scale = xa_dh ** -0.5
    outs = []
    for hd in range(xa_heads):
        cs = slice(hd * xa_dh, (hd + 1) * xa_dh)
        s = _nt(qx[:, cs].astype(bf16), mk_ref[0, :, cs].astype(bf16)) * scale
        p = jnp.exp(s - jnp.max(s, axis=-1, keepdims=True))
        p = p / jnp.sum(p, axis=-1, keepdims=True)
        outs.append(jnp.dot(p.astype(bf16), mv_ref[0, :, cs].astype(bf16), preferred_element_type=f32))
    o = jnp.concatenate(outs, axis=-1).astype(bf16)
    x2 = x1 + jnp.dot(o, wxo_ref[...], preferred_element_type=f32)
    x2_ref[...] = x2
    h2 = _rms(x2, gm_ref[...])
    tm = h2.shape[0]
    s = h2.shape[1] // LANES
    for c in range(s):
        h2_ref[pl.ds(c, tm, stride=s), :] = h2[:, c * LANES:(c + 1) * LANES]
    lg_ref[...] = jnp.dot(h2.astype(bf16), wr_ref[...], preferred_element_type=f32) + br_ref[...]


def _token_stage(cg, ag, x, mem_k, mem_v, w_out, g_x, w_xq, w_xo, g_moe, w_r, b_r, *, tm, rows_per_seq, xa_heads):
    n, d = x.shape
    s = d // LANES
    xa_w = w_xq.shape[1]
    n_mem = mem_k.shape[1]
    tm = min(tm, rows_per_seq)
    assert rows_per_seq % tm == 0 and n % tm == 0
    per = rows_per_seq // tm
    row = lambda i: (i, 0)
    const = lambda i: (0, 0)
    mem_spec = pl.BlockSpec((1, n_mem, xa_w), lambda i: (i // per, 0, 0))
    nl = w_r.shape[1]
    return pl.pallas_call(
        functools.partial(_token_body, xa_heads=xa_heads, xa_dh=xa_w // xa_heads),
        grid=(n // tm,),
        in_specs=[pl.BlockSpec((tm, d), row), pl.BlockSpec((tm, d), row), pl.BlockSpec((tm, d), row), mem_spec,
                  mem_spec, pl.BlockSpec((d, d), const), pl.BlockSpec((1, d), const), pl.BlockSpec((d, xa_w), const),
                  pl.BlockSpec((xa_w, d), const), pl.BlockSpec((1, d), const), pl.BlockSpec((d, nl), const),
                  pl.BlockSpec((1, nl), const)],
        out_specs=[pl.BlockSpec((tm, d), row), pl.BlockSpec((tm * s, LANES), row), pl.BlockSpec((tm, nl), row)],
        out_shape=[jax.ShapeDtypeStruct((n, d), f32), jax.ShapeDtypeStruct((n * s, LANES), f32),
                   jax.ShapeDtypeStruct((n, nl), f32)],
        compiler_params=_params(("parallel",)), name="token_stage",
    )(cg, ag, x, mem_k, mem_v, w_out, g_x.reshape(1, d), w_xq, w_xo, g_moe.reshape(1, d), w_r, b_r)


def _router_body(lg_ref, ri_ref, rg_ref, cnt_ref, carry_ref, *, n_exp, n_grp):
    tm, nl = lg_ref.shape
    epg = n_exp // n_grp

    @pl.when(pl.program_id(0) == 0)
    def _():
        carry_ref[...] = jnp.zeros_like(carry_ref)

    lg = lg_ref[...]
    lane = lax.broadcasted_iota(i32, (tm, nl), 1)
    is_grp = (lane >= n_exp) & (lane < n_exp + n_grp)
    gl = jnp.where(is_grp, lg, -jnp.inf)
    gmax = jnp.max(gl, axis=-1, keepdims=True)
    g_top = jnp.min(jnp.where(gl == gmax, lane - n_exp, BIG_I), axis=-1, keepdims=True)
    g_prob = 1.0 / jnp.sum(jnp.where(is_grp, jnp.exp(lg - gmax), 0.0), axis=-1, keepdims=True)
    in_grp = (lane >= g_top * epg) & (lane < (g_top + 1) * epg)
    el = jnp.where(in_grp, lg, -jnp.inf)
    m1 = jnp.max(el, axis=-1, keepdims=True)
    e1 = jnp.min(jnp.where(el == m1, lane, BIG_I), axis=-1, keepdims=True)
    el2 = jnp.where(lane == e1, -jnp.inf, el)
    m2 = jnp.max(el2, axis=-1, keepdims=True)
    e2 = jnp.min(jnp.where(el2 == m2, lane, BIG_I), axis=-1, keepdims=True)
    r21 = jnp.exp(m2 - m1)
    p1 = 1.0 / (1.0 + r21)
    gate1 = g_prob * p1
    gate2 = g_prob * (r21 * p1)

    hit1 = lane == e1
    hit2 = lane == e2
    onehot = jnp.where(hit1 | hit2, 1.0, 0.0)
    rr = lax.broadcasted_iota(i32, (tm, tm), 0)
    cc = lax.broadcasted_iota(i32, (tm, tm), 1)
    lower = jnp.where(cc < rr, 1.0, 0.0).astype(bf16)
    before = jnp.dot(lower, onehot.astype(bf16), preferred_element_type=f32) + carry_ref[0:1, :]
    rank1 = jnp.sum(jnp.where(hit1, before, 0.0), axis=-1, keepdims=True).astype(i32)
    rank2 = jnp.sum(jnp.where(hit2, before, 0.0), axis=-1, keepdims=True).astype(i32)
    carry_ref[0:1, :] = carry_ref[0:1, :] + jnp.sum(onehot, axis=0, keepdims=True)

    ri = jnp.where(lane == 0, e1, jnp.where(lane == 1, e2, jnp.where(lane == 2, rank1, jnp.where(lane == 3, rank2, 0))))
    ri_ref[...] = ri
    rg_ref[...] = jnp.where(lane == 0, gate1, jnp.where(lane == 1, gate2, 0.0))
    cnt_ref[...] = carry_ref[...]


def _router(logits, n_exp, n_grp, tm):
    n, nl = logits.shape
    tm = min(tm, n)
    assert n % tm == 0
    return pl.pallas_call(
        functools.partial(_router_body, n_exp=n_exp, n_grp=n_grp),
        grid=(n // tm,),
        in_specs=[pl.BlockSpec((tm, nl), lambda i: (i, 0))],
        out_specs=[pl.BlockSpec((tm, nl), lambda i: (i, 0)), pl.BlockSpec((tm, nl), lambda i: (i, 0)),
                   pl.BlockSpec((8, nl), lambda i: (0, 0))],
        out_shape=[jax.ShapeDtypeStruct((n, nl), i32), jax.ShapeDtypeStruct((n, nl), f32),
                   jax.ShapeDtypeStruct((8, nl), f32)],
        scratch_shapes=[pltpu.VMEM((8, nl), f32)],
        compiler_params=_params(("arbitrary",)), name="router",
    )(logits)


def _dispatch_body(slot_ref, h_ref, zero_ref, xs_ref, sem, *, tg, s):
    del zero_ref
    i = pl.program_id(0)

    def issue(r, carry):
        src = h_ref.at[pl.ds(pl.multiple_of(r * s, s), s), :]
        for kq in range(EXPERT_TOP_K):
            sl = slot_ref[(i * tg + r) * EXPERT_TOP_K + kq]
            pltpu.make_async_copy(src, xs_ref.at[pl.ds(pl.multiple_of(sl * s, s), s), :], sem).start()
        return carry

    lax.fori_loop(0, tg, issue, 0)
    for _ in range(EXPERT_TOP_K):
        pltpu.make_async_copy(h_ref, xs_ref.at[pl.ds(0, tg * s), :], sem).wait()


def _dispatch(h_rows, slot, n_slots, s, tg):
    n = h_rows.shape[0] // s
    tg = min(tg, n)
    assert n % tg == 0
    zeros = jnp.zeros((n_slots * s, LANES), f32)
    return pl.pallas_call(
        functools.partial(_dispatch_body, tg=tg, s=s),
        grid_spec=pltpu.PrefetchScalarGridSpec(
            num_scalar_prefetch=1, grid=(n // tg,),
            in_specs=[pl.BlockSpec((tg * s, LANES), lambda i, sl: (i, 0)), pl.BlockSpec(memory_space=pl.ANY)],
            out_specs=pl.BlockSpec(memory_space=pl.ANY),
            scratch_shapes=[pltpu.SemaphoreType.DMA(())]),
        out_shape=jax.ShapeDtypeStruct((n_slots * s, LANES), f32),
        input_output_aliases={2: 0},
        compiler_params=_params(("arbitrary",)), name="dispatch",
    )(slot, h_rows, zeros)


def _ffn_body(be_ref, nu_ref, x_ref, w1_ref, w3_ref, w2_ref, y_ref, xb_ref, w1b_ref, w3b_ref, w2b_ref, *, rb, s):
    i = pl.program_id(0)

    @pl.when(i < nu_ref[0])
    def _():
        @pl.when((i == 0) | (be_ref[i] != be_ref[jnp.maximum(i - 1, 0)]))
        def _():
            w1b_ref[...] = w1_ref[0].astype(bf16)
            w3b_ref[...] = w3_ref[0].astype(bf16)
            w2b_ref[...] = w2_ref[0].astype(bf16)

        for c in range(s):
            xb_ref[:, c * LANES:(c + 1) * LANES] = x_ref[pl.ds(c, rb, stride=s), :].astype(bf16)
        xb = xb_ref[...]
        a = jnp.dot(xb, w1b_ref[...], preferred_element_type=f32)
        g = jnp.dot(xb, w3b_ref[...], preferred_element_type=f32)
        mid = (a * _sigmoid(a) * g).astype(bf16)
        y = jnp.dot(mid, w2b_ref[...], preferred_element_type=f32)
        for c in range(s):
            y_ref[pl.ds(c, rb, stride=s), :] = y[:, c * LANES:(c + 1) * LANES]

    @pl.when(i >= nu_ref[0])
    def _():
        y_ref[...] = jnp.zeros_like(y_ref)


def _expert_ffn(xs_rows, blk_expert, n_used, w1, w3, w2, rb):
    n_exp, d, de = w1.shape
    s = d // LANES
    ns = xs_rows.shape[0] // s
    row = lambda i, be, nu: (i, 0)
    return pl.pallas_call(
        functools.partial(_ffn_body, rb=rb, s=s),
        grid_spec=pltpu.PrefetchScalarGridSpec(
            num_scalar_prefetch=2, grid=(ns // rb,),
            in_specs=[pl.BlockSpec((rb * s, LANES), row),
                      pl.BlockSpec((1, d, de), lambda i, be, nu: (be[i], 0, 0)),
                      pl.BlockSpec((1, d, de), lambda i, be, nu: (be[i], 0, 0)),
                      pl.BlockSpec((1, de, d), lambda i, be, nu: (be[i], 0, 0))],
            out_specs=pl.BlockSpec((rb * s, LANES), row),
            scratch_shapes=[pltpu.VMEM((rb, d), bf16), pltpu.VMEM((d, de), bf16), pltpu.VMEM((d, de), bf16),
                            pltpu.VMEM((de, d), bf16)]),
        out_shape=jax.ShapeDtypeStruct((ns * s, LANES), f32),
        compiler_params=_params(("arbitrary",)), name="expert_ffn",
    )(blk_expert, n_used, xs_rows, w1, w3, w2)


def _combine_body(slot_ref, x2_ref, rg_ref, gf_ref, ys_ref, o_ref, ybuf, sem, *, tc, s, n_steps):
    i = pl.program_id(0)
    cur = i % 2

    def start_gather(step, buf):
        def issue(r, carry):
            for kq in range(EXPERT_TOP_K):
                sl = slot_ref[(step * tc + r) * EXPERT_TOP_K + kq]
                pltpu.make_async_copy(ys_ref.at[pl.ds(pl.multiple_of(sl * s, s), s), :],
                                      ybuf.at[buf, kq, pl.ds(pl.multiple_of(r * s, s), s), :], sem.at[buf]).start()
            return carry

        lax.fori_loop(0, tc, issue, 0)

    @pl.when(i == 0)
    def _():
        start_gather(i, cur)

    @pl.when(i + 1 < n_steps)
    def _():
        start_gather(i + 1, 1 - cur)

    for kq in range(EXPERT_TOP_K):
        pltpu.make_async_copy(ys_ref.at[pl.ds(0, tc * s), :], ybuf.at[cur, kq], sem.at[cur]).wait()

    rg = rg_ref[...]
    lane = lax.broadcasted_iota(i32, rg.shape, 1)
    gates = [_round_bf16(jnp.sum(jnp.where(lane == kq, rg, 0.0), axis=-1, keepdims=True))
             for kq in range(EXPERT_TOP_K)]
    ss = jnp.zeros((tc, 1), f32)
    for c in range(s):
        cs = slice(c * LANES, (c + 1) * LANES)
        val = x2_ref[:, cs]
        for kq in range(EXPERT_TOP_K):
            val = val + gates[kq] * _round_bf16(ybuf[cur, kq, pl.ds(c, tc, stride=s), :])
        o_ref[:, cs] = val
        ss = ss + jnp.sum(val * val, axis=-1, keepdims=True)
    o_ref[...] = o_ref[...] * lax.rsqrt(ss * (1.0 / (s * LANES)) + RMS_EPS) * gf_ref[...]


def _combine(x2, gates, g_final, ys_rows, slot, tc):
    n, d = x2.shape
    s = d // LANES
    tc = min(tc, n)
    assert n % tc == 0
    return pl.pallas_call(
        functools.partial(_combine_body, tc=tc, s=s, n_steps=n // tc),
        grid_spec=pltpu.PrefetchScalarGridSpec(
            num_scalar_prefetch=1, grid=(n // tc,),
            in_specs=[pl.BlockSpec((tc, d), lambda i, sl: (i, 0)), pl.BlockSpec((tc, LANES), lambda i, sl: (i, 0)),
                      pl.BlockSpec((1, d), lambda i, sl: (0, 0)), pl.BlockSpec(memory_space=pl.ANY)],
            out_specs=pl.BlockSpec((tc, d), lambda i, sl: (i, 0)),
            scratch_shapes=[pltpu.VMEM((2, EXPERT_TOP_K, tc * s, LANES), f32), pltpu.SemaphoreType.DMA((2,))]),
        out_shape=jax.ShapeDtypeStruct((n, d), f32),
        compiler_params=_params(("arbitrary",)), name="combine",
    )(slot, x2, gates, g_final.reshape(1, d), ys_rows)


def _moe_and_final(x2, h_rows, logits, g_final, w1, w3, w2, n_grp, *, tm_route, rb, tg, tc):
    n, d = x2.shape
    n_exp = w1.shape[0]
    s = d // LANES
    ri, rg, cnt = _router(logits, n_exp, n_grp, tm_route)
    sizes = cnt[0, :n_exp].astype(i32)
    padded = (sizes + rb - 1) // rb * rb
    pad_end = jnp.cumsum(padded)
    pad_start = pad_end - padded
    slot = (pad_start[ri[:, :EXPERT_TOP_K]] + ri[:, EXPERT_TOP_K:2 * EXPERT_TOP_K]).reshape(-1).astype(i32)
    n_blocks = -(-(n * EXPERT_TOP_K) // rb) + n_exp
    blk_start = jnp.arange(n_blocks, dtype=i32) * rb
    blk_expert = jnp.minimum(jnp.sum((pad_end[None, :] <= blk_start[:, None]).astype(i32), axis=1), n_exp - 1)
    n_used = (pad_end[-1:] // rb).astype(i32)
    xs_rows = _dispatch(h_rows, slot, n_blocks * rb, s, tg)
    ys_rows = _expert_ffn(xs_rows, blk_expert, n_used, w1, w3, w2, rb)
    return _combine(x2, rg, g_final, ys_rows, slot, tc)


def _layer(l, xp, xs, cache_k, cache_v, cache_conv, cache_mem_k, cache_mem_v, page_table, mem_prompt, rel_table,
           g_mix, w_in, w_dw, b_dw, ln_g, ln_b, w_pw, w_out, g_mem, w_xk, w_xv, g_xattn, w_xq, w_xo, g_moe,
           w_rg, b_rg, w_re, b_re, w1, w3, w2, g_final, is_last):
    assert is_last, "the final norm is fused into the last layer's combine kernel"
    bp, s_len, d = xp.shape
    bs, t_len, _ = xs.shape
    page, n_heads, dh = cache_k.shape[2:]
    width, dc = w_dw.shape[1:]
    attn_w = n_heads * dh
    n_mem, xa_heads, xa_dh = cache_mem_k.shape[2:]
    xa_w = xa_heads * xa_dh
    n_grp, n_exp = w_rg.shape[-1], w_re.shape[-1]

    w_in_b = w_in[l].astype(bf16)
    w_pw_b, w_out_b = w_pw[l].astype(bf16), w_out[l].astype(bf16)
    w_xq_b, w_xo_b = w_xq[l].astype(bf16), w_xo[l].astype(bf16)
    w_mem_b = jnp.concatenate([w_xk[l], w_xv[l]], axis=1).astype(bf16)
    nl =-(-(n_exp + n_grp) // LANES) * LANES
    w_r = jnp.pad(jnp.concatenate([w_re[l], w_rg[l]], axis=1), ((0, 0), (0, nl - n_exp - n_grp))).astype(bf16)
    b_r = jnp.pad(jnp.concatenate([b_re[l], b_rg[l]]), (0, nl - n_exp - n_grp)).reshape(1, nl)
    pieces = [(0, 2 * dc), (2 * dc, attn_w), (2 * dc + attn_w, attn_w), (2 * dc + 2 * attn_w, attn_w),
              (2 * dc + 3 * attn_w, d), (2 * dc + 3 * attn_w + d, d)]

    n_p = bp * s_len
    glu, q, k, v, gc, ga = _norm_proj(xp.reshape(n_p, d), g_mix[l], w_in_b, pieces, tm=512)
    cg, conv_p = _conv_branch(glu.reshape(bp, s_len, 2 * dc), gc.reshape(bp, s_len, d), None, w_dw[l], b_dw[l],
                              ln_g[l], ln_b[l], w_pw_b, ts=256)
    ag = _moba_prompt(q.reshape(bp, s_len, attn_w), k.reshape(bp, s_len, attn_w), v.reshape(bp, s_len, attn_w),
                      ga.reshape(bp, s_len, attn_w), rel_table, page)
    mk, mv = _norm_proj(mem_prompt.reshape(bp * n_mem, d), g_mem[l], w_mem_b, [(0, xa_w), (xa_w, xa_w)], tm=512)
    x2, h2, logits = _token_stage(cg.reshape(n_p, d), ag.reshape(n_p, d), xp.reshape(n_p, d),
                                  mk.reshape(bp, n_mem, xa_w), mv.reshape(bp, n_mem, xa_w), w_out_b, g_xattn[l],
                                  w_xq_b, w_xo_b, g_moe[l], w_r, b_r, tm=256, rows_per_seq=s_len,
                                  xa_heads=xa_heads)
    y_p = _moe_and_final(x2, h2, logits, g_final, w1[l], w3[l], w2[l], n_grp, tm_route=256, rb=256, tg=256, tc=256)
    prompt_out = (y_p.reshape(bp, s_len, d), k.reshape(bp, s_len, n_heads, dh), v.reshape(bp, s_len, n_heads, dh),
                  conv_p, mk.reshape(bp, n_mem, xa_heads, xa_dh), mv.reshape(bp, n_mem, xa_heads, xa_dh))

    rows = SAMPLE_ROWS
    assert t_len <= rows
    n_s = bs * rows
    xs_pad = jnp.pad(xs, ((0, 0), (0, rows - t_len), (0, 0))).reshape(n_s, d)
    glu, q, k, v, gc, ga = _norm_proj(xs_pad, g_mix[l], w_in_b, pieces, tm=512)
    state = jnp.pad(cache_conv[l], ((0, 0), (HALO - (width - 1), 0), (0, 0)))
    cg, conv_s = _conv_branch(glu.reshape(bs, rows, 2 * dc), gc.reshape(bs, rows, d), state, w_dw[l], b_dw[l],
                              ln_g[l], ln_b[l], w_pw_b, ts=rows, t_real=t_len, out_dtype=f32)
    ag = _moba_sample(q.reshape(bs, rows, attn_w), k.reshape(bs, rows, attn_w), v.reshape(bs, rows, attn_w),
                      ga.reshape(bs, rows, attn_w), cache_k, cache_v, l, page_table, rel_table, t_len)
    x2, h2, logits = _token_stage(cg.reshape(n_s, d), ag.reshape(n_s, d), xs_pad,
                                  cache_mem_k[l].reshape(bs, n_mem, xa_w), cache_mem_v[l].reshape(bs, n_mem, xa_w),
                                  w_out_b, g_xattn[l], w_xq_b, w_xo_b, g_moe[l], w_r, b_r, tm=rows,
                                  rows_per_seq=rows, xa_heads=xa_heads)
    real = lambda a: a.reshape(bs, rows, -1)[:, :t_len].reshape(bs * t_len, -1)
    h2_real = h2.reshape(bs, rows, d)[:, :t_len].reshape(bs * t_len * (d // LANES), LANES)
    y_s = _moe_and_final(real(x2), h2_real, real(logits), g_final, w1[l], w3[l], w2[l], n_grp, tm_route=128, rb=16,
                         tg=128, tc=128)
    sample_out = (y_s.reshape(bs, t_len, d), real(k).reshape(bs, t_len, n_heads, dh),
                  real(v).reshape(bs, t_len, n_heads, dh), conv_s)
    return prompt_out, sample_out


def kernel(x_prompt, x_sample, cache_k, cache_v, cache_conv, cache_mem_k, cache_mem_v, page_table, mem_prompt, rel_table, g_mix, w_in, w_dw, b_dw, ln_g, ln_b, w_pw, w_out, g_mem, w_xk, w_xv, g_xattn, w_xq, w_xo, g_moe, w_rg, b_rg, w_re, b_re, w1, w3, w2, g_final):
    depth = w_in.shape[0]
    assert depth == 1, "one layer per step"
    (y_p, k_p, v_p, conv_p, mk_p, mv_p), (y_s, k_s, v_s, conv_s) = _layer(
        0, x_prompt, x_sample, cache_k, cache_v, cache_conv, cache_mem_k, cache_mem_v, page_table, mem_prompt,
        rel_table, g_mix, w_in, w_dw, b_dw, ln_g, ln_b, w_pw, w_out, g_mem, w_xk, w_xv, g_xattn, w_xq, w_xo, g_moe,
        w_rg, b_rg, w_re, b_re, w1, w3, w2, g_final, is_last=True)
    return (y_p, y_s, k_p[None], v_p[None], conv_p[None], mk_p[None], mv_p[None], k_s[None], v_s[None],
            conv_s[None])
```

```python
import functools
import math

import numpy as np
import jax
import jax.numpy as jnp
from jax import lax
from jax.experimental import pallas as pl
from jax.experimental.pallas import tpu as pltpu

f32 = jnp.float32
bf16 = jnp.bfloat16
i32 = jnp.int32

MOBA_BLOCK = 256
MOBA_TOP_K = 3
MAX_DISTANCE = 128
EXPERT_TOP_K = 2
RMS_EPS = 1e-6
LN_EPS = 1e-5

LANES = 128
SUBLANES = 8
SAMPLE_ROWS = 16
MASKED = -1e30
BIG_I = 1 << 20
LOG2E = 1.4426950408889634
VMEM_LIMIT = 56 * 1024 * 1024


def _params(sem, vmem=VMEM_LIMIT):
    return pltpu.CompilerParams(dimension_semantics=sem, vmem_limit_bytes=vmem)


def _sigmoid(x):
    return 1.0 / (1.0 + jnp.exp(-x))


def _rms(x, g, eps=RMS_EPS):
    return x * lax.rsqrt(jnp.mean(x * x, axis=-1, keepdims=True) + eps) * g


def _nt(a, b):
    return lax.dot_general(a, b, (((1,), (1,)), ((), ())), preferred_element_type=f32)


def _round_bf16(x):
    return x.astype(bf16).astype(f32)


def _t5_bucket_np(dist, n_buckets):
    dist = np.maximum(dist, 0)
    max_exact = n_buckets // 2
    ratio = (np.log(np.maximum(dist, 1).astype(np.float32) / np.float32(max_exact))
             / np.float32(math.log(MAX_DISTANCE / max_exact))).astype(np.float32)
    large = np.minimum(max_exact + (ratio * np.float32(n_buckets - max_exact)).astype(np.int32), n_buckets - 1)
    return np.where(dist < max_exact, dist, large).astype(np.int32)


def _norm_proj_body(x_ref, g_ref, *refs, n_out):
    w_refs, o_refs, hn_ref = refs[:n_out], refs[n_out:2 * n_out], refs[2 * n_out]

    @pl.when(pl.program_id(1) == 0)
    def _():
        hn_ref[...] = _rms(x_ref[...], g_ref[...]).astype(bf16)

    h = hn_ref[...]
    for w_ref, o_ref in zip(w_refs, o_refs):
        o_ref[...] = jnp.dot(h, w_ref[...], preferred_element_type=f32).astype(o_ref.dtype)


def _norm_proj_means_body(pt_ref, x_ref, g_ref, *refs, n_out, n_steps, layer, pps, ppb, page):
    w_refs, ck_ref = refs[:n_out], refs[n_out]
    o_refs, bm_ref = refs[n_out + 1:2 * n_out + 1], refs[2 * n_out + 1]
    pbuf, sem = refs[2 * n_out + 2:]
    step = pl.program_id(0) * pl.num_programs(1) + pl.program_id(1)
    slot = step % 2

    def copies(st, sl):
        return [pltpu.make_async_copy(ck_ref.at[layer, pt_ref[st * pps + p]], pbuf.at[sl, p], sem.at[sl])
                for p in range(pps)]

    @pl.when(step == 0)
    def _():
        for cp in copies(step, slot):
            cp.start()

    @pl.when(step + 1 < n_steps)
    def _():
        for cp in copies(step + 1, 1 - slot):
            cp.start()

    h = _rms(x_ref[...], g_ref[...]).astype(bf16)
    for w_ref, o_ref in zip(w_refs, o_refs):
        o_ref[...] = jnp.dot(h, w_ref[...], preferred_element_type=f32).astype(o_ref.dtype)

    for cp in copies(step, slot):
        cp.wait()
    for blk_i in range(pps // ppb):
        acc = jnp.zeros(bm_ref.shape[2:], f32)
        for p in range(ppb):
            acc = acc + jnp.sum(pbuf[slot, blk_i * ppb + p], axis=0) * (1.0 / page)
        bm_ref[0, blk_i] = acc * (1.0 / ppb)


def _norm_proj(x, g, w, pieces, tm):
    m, k = x.shape
    tm = min(tm, m)
    assert m % tm == 0
    min_w = min(wd for _, wd in pieces)
    nj = max(1, min_w // 256)
    in_specs = [pl.BlockSpec((tm, k), lambda i, j: (i, 0)), pl.BlockSpec((1, k), lambda i, j: (0, 0))]
    out_specs, out_shape = [], []
    for off, wd in pieces:
        assert wd % nj == 0 and off % (wd // nj) == 0
        tn = wd // nj
        in_specs.append(pl.BlockSpec((k, tn), functools.partial(lambda i, j, o: (0, o + j), o=off // tn)))
        out_specs.append(pl.BlockSpec((tm, tn), lambda i, j: (i, j)))
        out_shape.append(jax.ShapeDtypeStruct((m, wd), f32))
    return pl.pallas_call(
        functools.partial(_norm_proj_body, n_out=len(pieces)),
        grid=(m // tm, nj), in_specs=in_specs, out_specs=out_specs, out_shape=out_shape,
        scratch_shapes=[pltpu.VMEM((tm, k), bf16)],
        compiler_params=_params(("parallel", "arbitrary")), name="norm_proj",
    )(x, g.reshape(1, k), *([w] * len(pieces)))


def _block_means_fit_under(m, tm, pieces, page_table_shape, page):
    nj = min(wd for _, wd in pieces) // 256
    if nj < 1 or m % tm:
        return False
    bs, n_pages = page_table_shape
    n_steps = (m // tm) * nj
    if (bs * n_pages) % n_steps:
        return False
    pps = bs * n_pages // n_steps
    return pps % (MOBA_BLOCK // page) == 0 and n_pages % pps == 0


def _norm_proj_with_block_means(x, g, w, pieces, tm, cache_k, layer, page_table):
    m, k = x.shape
    assert m % tm == 0
    nj = min(wd for _, wd in pieces) // 256
    n_steps = (m // tm) * nj
    _, _, page, n_heads, dh = cache_k.shape
    bs, n_pages = page_table.shape
    ppb = MOBA_BLOCK // page
    n_blocks = n_pages // ppb
    assert n_pages % ppb == 0 and (bs * n_pages) % n_steps == 0
    pps = bs * n_pages // n_steps
    assert pps % ppb == 0 and n_pages % pps == 0
    bps, spb = pps // ppb, n_pages // pps
    ni = m // tm
    in_specs = [pl.BlockSpec((tm, k), lambda j, i, pt: (i, 0)), pl.BlockSpec((1, k), lambda j, i, pt: (0, 0))]
    out_specs, out_shape = [], []
    for off, wd in pieces:
        assert wd % nj == 0 and off % (wd // nj) == 0
        tn = wd // nj
        in_specs.append(pl.BlockSpec((k, tn), functools.partial(lambda j, i, pt, o: (0, o + j), o=off // tn)))
        out_specs.append(pl.BlockSpec((tm, tn), lambda j, i, pt: (i, j)))
        out_shape.append(jax.ShapeDtypeStruct((m, wd), f32))
    in_specs.append(pl.BlockSpec(memory_space=pl.ANY))
    out_specs.append(pl.BlockSpec((1, bps, n_heads, dh),
                                  lambda j, i, pt: ((j * ni + i) // spb, (j * ni + i) % spb, 0, 0)))
    out_shape.append(jax.ShapeDtypeStruct((bs, n_blocks, n_heads, dh), f32))
    outs = pl.pallas_call(
        functools.partial(_norm_proj_means_body, n_out=len(pieces), n_steps=n_steps, layer=layer, pps=pps,
                          ppb=ppb, page=page),
        grid_spec=pltpu.PrefetchScalarGridSpec(
            num_scalar_prefetch=1, grid=(nj, ni), in_specs=in_specs, out_specs=out_specs,
            scratch_shapes=[pltpu.VMEM((2, pps, page, n_heads, dh), f32), pltpu.SemaphoreType.DMA((2,))]),
        out_shape=out_shape,
        compiler_params=_params(("arbitrary", "arbitrary")), name="norm_proj_means",
    )(page_table.reshape(-1), x, g.reshape(1, k), *([w] * len(pieces)), cache_k)
    return outs[:-1], outs[-1]


HALO = 32


def _conv_body(glu_ref, halo_ref, gc_ref, wdw_ref, bdw_ref, lng_ref, lnb_ref, wpw_ref, cg_ref, st_ref,
               hist_ref, histr_ref, y_ref, *, dc, width, ts, t_real, halo_is_state, n_tiles):
    i = pl.program_id(1)
    glu = glu_ref[0]
    hist_ref[HALO:HALO + ts, :] = glu[:, :dc] * _sigmoid(glu[:, dc:])
    if halo_is_state:
        hist_ref[0:HALO, :] = halo_ref[0]
    else:
        hl = halo_ref[0]
        uh = hl[:, :dc] * _sigmoid(hl[:, dc:])
        hist_ref[0:HALO, :] = jnp.where(i == 0, 0.0, uh)
    histr_ref[0:HALO + ts, :] = _round_bf16(hist_ref[0:HALO + ts, :])
    histr_ref[HALO + ts:HALO + ts + SUBLANES, :] = jnp.zeros((SUBLANES, dc), f32)
    base = HALO - (width - 1)
    rs = min(ts, 128)
    for c in range(dc // LANES):
        cs = slice(c * LANES, (c + 1) * LANES)
        for r0 in range(0, ts, rs):
            acc = bdw_ref[:, cs]
            for r in range(SUBLANES):
                part = None
                for w in range(width):
                    if (base + w) % SUBLANES == r:
                        lo = r0 + base + w - r
                        term = histr_ref[lo:lo + rs + SUBLANES, cs] * wdw_ref[w:w + 1, cs]
                        part = term if part is None else part + term
                if part is not None:
                    acc = acc + part[r:r + rs]
            y_ref[r0:r0 + rs, cs] = acc
    y = y_ref[...]
    yc = y - jnp.mean(y, axis=-1, keepdims=True)
    var = jnp.mean(yc * yc, axis=-1, keepdims=True)
    z = yc * lax.rsqrt(var + LN_EPS) * lng_ref[...] + lnb_ref[...]
    z = z * _sigmoid(z)
    conv_out = jnp.dot(z.astype(bf16), wpw_ref[...], preferred_element_type=f32)
    cg_ref[0] = (_sigmoid(gc_ref[0]) * conv_out).astype(cg_ref.dtype)

    @pl.when(i == n_tiles - 1)
    def _():
        last = HALO + (ts if t_real is None else t_real)
        st_ref[0] = hist_ref[last - (width - 1):last, :]


def _conv_branch(glu, gc, state, w_dw, b_dw, ln_g, ln_b, w_pw, *, ts, t_real=None, out_dtype=bf16):
    b, t, _ = glu.shape
    width, dc = w_dw.shape
    d = w_pw.shape[1]
    ts = min(ts, t)
    assert t % ts == 0 and ts % HALO == 0 or t == ts
    assert width - 1 <= HALO and dc % LANES == 0
    n_tiles = t // ts
    halo_is_state = state is not None
    if halo_is_state:
        assert n_tiles == 1
        halo_arr = state
        halo_spec = pl.BlockSpec((1, HALO, dc), lambda bi, i: (bi, 0, 0))
    else:
        assert t >= width - 1
        halo_arr = glu
        r = ts // HALO
        halo_spec = pl.BlockSpec((1, HALO, 2 * dc), lambda bi, i: (bi, jnp.maximum(i * r - 1, 0), 0))
    const = lambda bi, i: (0, 0)
    return pl.pallas_call(
        functools.partial(_conv_body, dc=dc, width=width, ts=ts, t_real=t_real, halo_is_state=halo_is_state,
                          n_tiles=n_tiles),
        grid=(b, n_tiles),
        in_specs=[pl.BlockSpec((1, ts, 2 * dc), lambda bi, i: (bi, i, 0)), halo_spec,
                  pl.BlockSpec((1, ts, d), lambda bi, i: (bi, i, 0)),
                  pl.BlockSpec((width, dc), const), pl.BlockSpec((1, dc), const), pl.BlockSpec((1, dc), const),
                  pl.BlockSpec((1, dc), const), pl.BlockSpec((dc, d), const)],
        out_specs=[pl.BlockSpec((1, ts, d), lambda bi, i: (bi, i, 0)),
                   pl.BlockSpec((1, width - 1, dc), lambda bi, i: (bi, 0, 0))],
        out_shape=[jax.ShapeDtypeStruct((b, t, d), out_dtype), jax.ShapeDtypeStruct((b, width - 1, dc), f32)],
        scratch_shapes=[pltpu.VMEM((HALO + ts, dc), f32), pltpu.VMEM((HALO + ts + SUBLANES, dc), f32),
                        pltpu.VMEM((ts, dc), f32)],
        compiler_params=_params(("parallel", "arbitrary")), name="conv_branch",
    )(glu, halo_arr, gc, w_dw, b_dw.reshape(1, dc), ln_g.reshape(1, dc), ln_b.reshape(1, dc), w_pw)


def _bias_body(tab_ref, map_ref, o_ref, *, n_buckets):
    h = pl.program_id(0)
    bm = map_ref[...]
    acc = jnp.zeros(bm.shape, f32)
    for bk in range(n_buckets):
        acc = jnp.where(bm == bk, tab_ref[bk, h], acc)
    o_ref[0] = acc


def _bias_tiles(rel_table, bucket_map):
    n_buckets, n_heads = rel_table.shape
    r, c = bucket_map.shape
    return pl.pallas_call(
        functools.partial(_bias_body, n_buckets=n_buckets),
        grid=(n_heads,),
        in_specs=[pl.BlockSpec(memory_space=pltpu.SMEM), pl.BlockSpec((r, c), lambda h: (0, 0))],
        out_specs=pl.BlockSpec((1, r, c), lambda h: (h, 0, 0)),
        out_shape=jax.ShapeDtypeStruct((n_heads, r, c), f32),
        compiler_params=_params(("parallel",)), name="bias_tiles",
    )(rel_table, jnp.asarray(bucket_map))


def _moba_prompt_body(tab_ref, q_ref, k_ref, v_ref, ga_ref, bias_ref, o_ref, vt_ref, kb_ref, sel_ref, s_ref, *,
                      s_len, dh, page, n_buckets):
    blk = MOBA_BLOCK
    nb = s_len // blk
    ppb = blk // page
    h = pl.program_id(1)
    c2 = dh ** -0.5 * LOG2E
    far2 = tab_ref[n_buckets - 1, h] * LOG2E

    rows = []
    for n in range(nb):
        acc = jnp.zeros((1, dh), f32)
        for p in range(ppb):
            acc = acc + jnp.sum(k_ref[0, n * blk + p * page:n * blk + (p + 1) * page, :], axis=0,
                                keepdims=True) * (1.0 / page)
        rows.append(acc * (1.0 / ppb))
        vt_ref[n] = jnp.transpose(v_ref[0, n * blk:(n + 1) * blk, :]).astype(bf16)
        kb_ref[n] = k_ref[0, n * blk:(n + 1) * blk, :].astype(bf16)
    bmean = jnp.concatenate(rows, axis=0)
    sc = _nt(_round_bf16(bmean), _round_bf16(q_ref[0]))
    n_idx = lax.broadcasted_iota(i32, (nb, s_len), 0)
    own = lax.broadcasted_iota(i32, (nb, s_len), 1) // blk
    valid = n_idx < own
    scm = jnp.where(valid, sc, -jnp.inf)
    rank = jnp.zeros((nb, s_len), f32)
    for m in range(nb - 1):
        row = scm[m:m + 1, :]
        beats = (row > scm) | ((row == scm) & (m < n_idx))
        rank = rank + jnp.where(beats, 1.0, 0.0)
    sel_add = jnp.where(valid & (rank < MOBA_TOP_K), 0.0, MASKED)
    for j in range(nb):
        sel_ref[j] = sel_add[:, j * blk:(j + 1) * blk]

    kk = lax.broadcasted_iota(i32, (blk, blk), 0)
    qq = lax.broadcasted_iota(i32, (blk, blk), 1)
    own_b = jnp.where(kk <= qq, bias_ref[0, 0:blk, :] * LOG2E, MASKED)
    adj_b = bias_ref[0, blk:2 * blk, :] * LOG2E

    for j in range(nb):
        qj = q_ref[0, j * blk:(j + 1) * blk, :].astype(bf16)
        blocks = [j] + ([j - 1] if j >= 1 else []) + list(range(j - 1))
        m = None
        for t, n in enumerate(blocks):
            if n == j:
                add = own_b
            elif n == j - 1:
                add = adj_b + sel_ref[j, n:n + 1, :]
            else:
                add = far2 + sel_ref[j, n:n + 1, :]
            s = _nt(kb_ref[n], qj) * c2 + add
            s_ref[t] = s
            mt = jnp.max(s, axis=0, keepdims=True)
            m = mt if m is None else jnp.maximum(m, mt)
        l = jnp.zeros((1, blk), f32)
        acc = jnp.zeros((dh, blk), f32)
        for t, n in enumerate(blocks):
            p = jnp.exp2(s_ref[t] - m)
            l = l + jnp.sum(p, axis=0, keepdims=True)
            acc = acc + jnp.dot(vt_ref[n], p.astype(bf16), preferred_element_type=f32)
        out = jnp.transpose(acc * (1.0 / l))
        gate = _sigmoid(ga_ref[0, j * blk:(j + 1) * blk, :])
        o_ref[0, j * blk:(j + 1) * blk, :] = (gate * out).astype(o_ref.dtype)


def _moba_prompt(q, k, v, ga, rel_table, page):
    b, s_len, width = q.shape
    n_buckets, n_heads = rel_table.shape
    dh = width // n_heads
    blk = MOBA_BLOCK
    assert s_len % blk == 0 and blk % page == 0 and dh % LANES == 0
    assert blk + 1 >= MAX_DISTANCE, "blocks two or more away must share the last bucket"
    kk = np.arange(blk)[:, None]
    qq = np.arange(blk)[None, :]
    bmap = np.concatenate([_t5_bucket_np(qq - kk, n_buckets), _t5_bucket_np(qq - kk + blk, n_buckets)], axis=0)
    bias = _bias_tiles(rel_table, bmap)
    nb = s_len // blk
    qkv_spec = pl.BlockSpec((1, s_len, dh), lambda bi, hi: (bi, 0, hi))
    return pl.pallas_call(
        functools.partial(_moba_prompt_body, s_len=s_len, dh=dh, page=page, n_buckets=n_buckets),
        grid=(b, n_heads),
        in_specs=[pl.BlockSpec(memory_space=pltpu.SMEM), qkv_spec, qkv_spec, qkv_spec, qkv_spec,
                  pl.BlockSpec((1, 2 * blk, blk), lambda bi, hi: (hi, 0, 0))],
        out_specs=pl.BlockSpec((1, s_len, dh), lambda bi, hi: (bi, 0, hi)),
        out_shape=jax.ShapeDtypeStruct((b, s_len, width), bf16),
        scratch_shapes=[pltpu.VMEM((nb, dh, blk), bf16), pltpu.VMEM((nb, blk, dh), bf16),
                        pltpu.VMEM((nb, nb, blk), f32), pltpu.VMEM((nb, blk, blk), f32)],
        compiler_params=_params(("parallel", "parallel")), name="moba_prompt",
    )(rel_table, q, k, v, ga, bias)


def _page_means_body(pt_ref, *refs, n_in, page, ppb):
    page_refs, o_ref = refs[:n_in], refs[n_in]
    for blk_i in range(n_in // ppb):
        acc = jnp.zeros(o_ref.shape[2:], f32)
        for p in range(ppb):
            acc = acc + jnp.sum(page_refs[blk_i * ppb + p][...], axis=0) * (1.0 / page)
        o_ref[0, blk_i] = acc * (1.0 / ppb)


def _block_means(cache_k, layer, page_table, n_blocks):
    _, _, page, n_heads, dh = cache_k.shape
    bs, n_pages = page_table.shape
    ppb = MOBA_BLOCK // page
    bps = math.gcd(n_blocks, 4)
    n_in = bps * ppb

    def page_spec(p):
        return pl.BlockSpec((None, None, page, n_heads, dh),
                            lambda b, c, pt: (layer, pt[b * n_pages + c * n_in + p], 0, 0, 0))

    return pl.pallas_call(
        functools.partial(_page_means_body, n_in=n_in, page=page, ppb=ppb),
        grid_spec=pltpu.PrefetchScalarGridSpec(
            num_scalar_prefetch=1, grid=(bs, n_blocks // bps),
            in_specs=[page_spec(p) for p in range(n_in)],
            out_specs=pl.BlockSpec((1, bps, n_heads, dh), lambda b, c, pt: (b, c, 0, 0))),
        out_shape=jax.ShapeDtypeStruct((bs, n_blocks, n_heads, dh), f32),
        compiler_params=_params(("parallel", "parallel")), name="block_means",
    )(page_table.reshape(-1), *([cache_k] * n_in))


def _select_body(q_ref, bm_ref, o_ref, *, n_top, n_heads, dh):
    out_lane = lax.broadcasted_iota(i32, o_ref.shape[2:], 1)
    for h in range(n_heads):
        cs = slice(h * dh, (h + 1) * dh)
        sc = _nt(_round_bf16(q_ref[0, :, cs]), _round_bf16(bm_ref[0, :, cs]))
        lane = lax.broadcasted_iota(i32, sc.shape, 1)
        out = jnp.zeros(o_ref.shape[2:], i32)
        for kq in range(n_top):
            best = jnp.max(sc, axis=-1, keepdims=True)
            idx = jnp.min(jnp.where(sc == best, lane, BIG_I), axis=-1, keepdims=True)
            out = jnp.where(out_lane == kq, idx, out)
            sc = jnp.where(lane == idx, -jnp.inf, sc)
        o_ref[0, h] = out


def _select_blocks(q, bmeans, n_heads, n_top):
    bs, rows, width = q.shape
    dh = width // n_heads
    n_blocks = bmeans.shape[1]
    return pl.pallas_call(
        functools.partial(_select_body, n_top=n_top, n_heads=n_heads, dh=dh),
        grid=(bs,),
        in_specs=[pl.BlockSpec((1, rows, width), lambda b: (b, 0, 0)),
                  pl.BlockSpec((1, n_blocks, width), lambda b: (b, 0, 0))],
        out_specs=pl.BlockSpec((1, n_heads, rows, LANES), lambda b: (b, 0, 0, 0)),
        out_shape=jax.ShapeDtypeStruct((bs, n_heads, rows, LANES), i32),
        compiler_params=_params(("parallel",)), name="select_blocks",
    )(q, bmeans)


def _moba_sample_body(phys_ref, sel_ref, tab_ref, ck_ref, cv_ref, q_ref, kn_ref, vn_ref, ga_ref, adj_ref, own_ref,
                      o_ref, kbuf, vbuf, sem, *, layer, t_len, n_top, ppb, page, n_heads, n_full, n_buckets, dh,
                      n_steps):
    b, h = pl.program_id(0), pl.program_id(1)
    blk = ppb * page
    scale = dh ** -0.5
    far_bias = tab_ref[n_buckets - 1, h]
    n_pg = n_top * ppb
    step = b * n_heads + h
    slot = step % 2

    def copies(cb, ch, sl, t, c):
        pg = phys_ref[((cb * t_len + t) * n_heads + ch) * n_pg + c]
        dst = pl.ds(c * page, page)
        return (pltpu.make_async_copy(ck_ref.at[layer, pg, :, ch, :], kbuf.at[sl, t, dst, :], sem.at[sl, 0]),
                pltpu.make_async_copy(cv_ref.at[layer, pg, :, ch, :], vbuf.at[sl, t, dst, :], sem.at[sl, 1]))

    def start_all(cb, ch, sl):
        for t in range(t_len):
            for c in range(n_pg):
                ck, cv = copies(cb, ch, sl, t, c)
                ck.start()
                cv.start()

    @pl.when(step == 0)
    def _():
        start_all(b, h, slot)

    @pl.when(step + 1 < n_steps)
    def _():
        nxt = step + 1
        start_all(nxt // n_heads, nxt % n_heads, 1 - slot)

    for t in range(t_len):
        for c in range(n_pg):
            ck, cv = copies(b, h, slot, t, c)
            ck.wait()
            cv.wait()

    rows = q_ref.shape[1]
    row_id = lax.broadcasted_iota(i32, (rows, 1), 0)
    out = jnp.zeros((rows, dh), f32)
    kn = _round_bf16(kn_ref[0])
    vn = _round_bf16(vn_ref[0])
    for t in range(t_len):
        qt = _round_bf16(q_ref[0, t:t + 1, :])
        s_sel = jnp.sum(_round_bf16(kbuf[slot, t]) * qt, axis=-1, keepdims=True) * scale
        biases = []
        for c in range(n_top):
            sblk = sel_ref[((b * t_len + t) * n_heads + h) * n_top + c]
            biases.append(jnp.where(sblk == n_full - 1, adj_ref[0, t * blk:(t + 1) * blk, :], far_bias))
        s_sel = s_sel + jnp.concatenate(biases, axis=0)
        s_own = jnp.sum(kn * qt, axis=-1, keepdims=True) * scale + own_ref[0, t * rows:(t + 1) * rows, :]
        s_own = jnp.where(row_id <= t, s_own, MASKED)
        m = jnp.maximum(jnp.max(s_sel, axis=0, keepdims=True), jnp.max(s_own, axis=0, keepdims=True))
        p_sel = jnp.exp(s_sel - m)
        p_own = jnp.exp(s_own - m)
        l = jnp.sum(p_sel, axis=0, keepdims=True) + jnp.sum(p_own, axis=0, keepdims=True)
        p_sel = _round_bf16(p_sel / l)
        p_own = _round_bf16(p_own / l)
        o_t = (jnp.sum(p_sel * _round_bf16(vbuf[slot, t]), axis=0, keepdims=True)
               + jnp.sum(p_own * vn, axis=0, keepdims=True))
        out = jnp.where(row_id == t, o_t, out)
    o_ref[0] = (_sigmoid(ga_ref[0]) * out).astype(o_ref.dtype)


def _moba_sample(q, k_new, v_new, ga, cache_k, cache_v, layer, page_table, rel_table, t_len, bmeans=None):
    bs, rows, width = q.shape
    _, _, page, n_heads, dh = cache_k.shape
    n_buckets = rel_table.shape[0]
    n_pages = page_table.shape[1]
    blk = MOBA_BLOCK
    ppb = blk // page
    past = n_pages * page
    n_full = past // blk
    assert past % blk == 0, "the new tokens must start a fresh MoBA block"
    assert t_len <= rows and t_len <= blk // 2 and blk + 1 >= MAX_DISTANCE
    n_top = min(MOBA_TOP_K, n_full)
    assert n_top >= 1
    if bmeans is None:
        bmeans = _block_means(cache_k, layer, page_table, n_full)
    sel = _select_blocks(q, bmeans.reshape(bs, n_full, width), n_heads, n_top)
    sel = jnp.transpose(sel[:, :, :t_len, :n_top], (0, 2, 1, 3))
    logical = sel[..., None] * ppb + jnp.arange(ppb, dtype=i32)
    phys = jnp.take_along_axis(page_table[:, None, None, :], logical.reshape(bs, t_len, n_heads, n_top * ppb),
                               axis=-1)
    tt = np.arange(8)[:, None]
    adj_map = _t5_bucket_np(blk + tt - np.arange(blk)[None, :], n_buckets)
    own_map = _t5_bucket_np(tt - np.arange(LANES)[None, :], n_buckets)
    tiles = _bias_tiles(rel_table, np.concatenate([adj_map, own_map], axis=1))
    adj = tiles[:, :t_len, :blk].reshape(n_heads, t_len * blk, 1)
    own = tiles[:, :t_len, blk:blk + rows].reshape(n_heads, t_len * rows, 1)
    tok_spec = pl.BlockSpec((1, rows, dh), lambda b, h, *_: (b, 0, h))
    return pl.pallas_call(
        functools.partial(_moba_sample_body, layer=layer, t_len=t_len, n_top=n_top, ppb=ppb, page=page,
                          n_heads=n_heads, n_full=n_full, n_buckets=n_buckets, dh=dh, n_steps=bs * n_heads),
        grid_spec=pltpu.PrefetchScalarGridSpec(
            num_scalar_prefetch=2, grid=(bs, n_heads),
            in_specs=[pl.BlockSpec(memory_space=pltpu.SMEM), pl.BlockSpec(memory_space=pl.ANY),
                      pl.BlockSpec(memory_space=pl.ANY), tok_spec, tok_spec, tok_spec, tok_spec,
                      pl.BlockSpec((1, t_len * blk, 1), lambda b, h, *_: (h, 0, 0)),
                      pl.BlockSpec((1, t_len * rows, 1), lambda b, h, *_: (h, 0, 0))],
            out_specs=pl.BlockSpec((1, rows, dh), lambda b, h, *_: (b, 0, h)),
            scratch_shapes=[pltpu.VMEM((2, t_len, n_top * blk, dh), f32),
                            pltpu.VMEM((2, t_len, n_top * blk, dh), f32), pltpu.SemaphoreType.DMA((2, 2))]),
        out_shape=jax.ShapeDtypeStruct((bs, rows, width), f32),
        compiler_params=_params(("arbitrary", "arbitrary")), name="moba_sample",
    )(phys.reshape(-1).astype(i32), sel.reshape(-1).astype(i32), rel_table, cache_k, cache_v, q, k_new, v_new, ga,
      adj, own)


def _token_body(cg_ref, ag_ref, x_ref, mk_ref, mv_ref, wout_ref, gx_ref, wxq_ref, wxo_ref, gm_ref, wr_ref, br_ref,
                x2_ref, h2_ref, lg_ref, *, xa_heads, xa_dh):
    merged = (cg_ref[...].astype(f32) + ag_ref[...].astype(f32)).astype(bf16)
    x1 = x_ref[...] + jnp.dot(merged, wout_ref[...], preferred_element_type=f32)
    hx = _rms(x1, gx_ref[...]).astype(bf16)
    qx = jnp.dot(hx, wxq_ref[...], preferred_element_type=f32)
    scale = xa_dh ** -0.5
    outs = []
    for hd in range(xa_heads):
        cs = slice(hd * xa_dh, (hd + 1) * xa_dh)
        s = _nt(qx[:, cs].astype(bf16), mk_ref[0, :, cs].astype(bf16)) * scale
        p = jnp.exp(s - jnp.max(s, axis=-1, keepdims=True))
        p = p / jnp.sum(p, axis=-1, keepdims=True)
        outs.append(jnp.dot(p.astype(bf16), mv_ref[0, :, cs].astype(bf16), preferred_element_type=f32))
    o = jnp.concatenate(outs, axis=-1).astype(bf16)
    x2 = x1 + jnp.dot(o, wxo_ref[...], preferred_element_type=f32)
    x2_ref[...] = x2
    h2 = _rms(x2, gm_ref[...])
    tm = h2.shape[0]
    s = h2.shape[1] // LANES
    for c in range(s):
        h2_ref[pl.ds(c, tm, stride=s), :] = h2[:, c * LANES:(c + 1) * LANES]
    lg_ref[...] = jnp.dot(h2.astype(bf16), wr_ref[...], preferred_element_type=f32) + br_ref[...]


def _token_stage(cg, ag, x, mem_k, mem_v, w_out, g_x, w_xq, w_xo, g_moe, w_r, b_r, *, tm, rows_per_seq, xa_heads):
    n, d = x.shape
    s = d // LANES
    xa_w = w_xq.shape[1]
    n_mem = mem_k.shape[1]
    tm = min(tm, rows_per_seq)
    assert rows_per_seq % tm == 0 and n % tm == 0
    per = rows_per_seq // tm
    row = lambda i: (i, 0)
    const = lambda i: (0, 0)
    mem_spec = pl.BlockSpec((1, n_mem, xa_w), lambda i: (i // per, 0, 0))
    nl = w_r.shape[1]
    return pl.pallas_call(
        functools.partial(_token_body, xa_heads=xa_heads, xa_dh=xa_w // xa_heads),
        grid=(n // tm,),
        in_specs=[pl.BlockSpec((tm, d), row), pl.BlockSpec((tm, d), row), pl.BlockSpec((tm, d), row), mem_spec,
                  mem_spec, pl.BlockSpec((d, d), const), pl.BlockSpec((1, d), const), pl.BlockSpec((d, xa_w), const),
                  pl.BlockSpec((xa_w, d), const), pl.BlockSpec((1, d), const), pl.BlockSpec((d, nl), const),
                  pl.BlockSpec((1, nl), const)],
        out_specs=[pl.BlockSpec((tm, d), row), pl.BlockSpec((tm * s, LANES), row), pl.BlockSpec((tm, nl), row)],
        out_shape=[jax.ShapeDtypeStruct((n, d), f32), jax.ShapeDtypeStruct((n * s, LANES), f32),
                   jax.ShapeDtypeStruct((n, nl), f32)],
        compiler_params=_params(("parallel",)), name="token_stage",
    )(cg, ag, x, mem_k, mem_v, w_out, g_x.reshape(1, d), w_xq, w_xo, g_moe.reshape(1, d), w_r, b_r)


def _router_body(lg_ref, ri_ref, rg_ref, cnt_ref, carry_ref, *, n_exp, n_grp):
    tm, nl = lg_ref.shape
    epg = n_exp // n_grp

    @pl.when(pl.program_id(0) == 0)
    def _():
        carry_ref[...] = jnp.zeros_like(carry_ref)

    lg = lg_ref[...]
    lane = lax.broadcasted_iota(i32, (tm, nl), 1)
    is_grp = (lane >= n_exp) & (lane < n_exp + n_grp)
    gl = jnp.where(is_grp, lg, -jnp.inf)
    gmax = jnp.max(gl, axis=-1, keepdims=True)
    g_top = jnp.min(jnp.where(gl == gmax, lane - n_exp, BIG_I), axis=-1, keepdims=True)
    g_prob = 1.0 / jnp.sum(jnp.where(is_grp, jnp.exp(lg - gmax), 0.0), axis=-1, keepdims=True)
    in_grp = (lane >= g_top * epg) & (lane < (g_top + 1) * epg)
    el = jnp.where(in_grp, lg, -jnp.inf)
    m1 = jnp.max(el, axis=-1, keepdims=True)
    e1 = jnp.min(jnp.where(el == m1, lane, BIG_I), axis=-1, keepdims=True)
    el2 = jnp.where(lane == e1, -jnp.inf, el)
    m2 = jnp.max(el2, axis=-1, keepdims=True)
    e2 = jnp.min(jnp.where(el2 == m2, lane, BIG_I), axis=-1, keepdims=True)
    r21 = jnp.exp(m2 - m1)
    p1 = 1.0 / (1.0 + r21)
    gate1 = g_prob * p1
    gate2 = g_prob * (r21 * p1)

    hit1 = lane == e1
    hit2 = lane == e2
    onehot = jnp.where(hit1 | hit2, 1.0, 0.0)
    rr = lax.broadcasted_iota(i32, (tm, tm), 0)
    cc = lax.broadcasted_iota(i32, (tm, tm), 1)
    lower = jnp.where(cc < rr, 1.0, 0.0).astype(bf16)
    before = jnp.dot(lower, onehot.astype(bf16), preferred_element_type=f32) + carry_ref[0:1, :]
    rank1 = jnp.sum(jnp.where(hit1, before, 0.0), axis=-1, keepdims=True).astype(i32)
    rank2 = jnp.sum(jnp.where(hit2, before, 0.0), axis=-1, keepdims=True).astype(i32)
    carry_ref[0:1, :] = carry_ref[0:1, :] + jnp.sum(onehot, axis=0, keepdims=True)

    ri = jnp.where(lane == 0, e1, jnp.where(lane == 1, e2, jnp.where(lane == 2, rank1, jnp.where(lane == 3, rank2, 0))))
    ri_ref[...] = ri
    rg_ref[...] = jnp.where(lane == 0, gate1, jnp.where(lane == 1, gate2, 0.0))
    cnt_ref[...] = carry_ref[...]


def _router(logits, n_exp, n_grp, tm):
    n, nl = logits.shape
    tm = min(tm, n)
    assert n % tm == 0
    return pl.pallas_call(
        functools.partial(_router_body, n_exp=n_exp, n_grp=n_grp),
        grid=(n // tm,),
        in_specs=[pl.BlockSpec((tm, nl), lambda i: (i, 0))],
        out_specs=[pl.BlockSpec((tm, nl), lambda i: (i, 0)), pl.BlockSpec((tm, nl), lambda i: (i, 0)),
                   pl.BlockSpec((8, nl), lambda i: (0, 0))],
        out_shape=[jax.ShapeDtypeStruct((n, nl), i32), jax.ShapeDtypeStruct((n, nl), f32),
                   jax.ShapeDtypeStruct((8, nl), f32)],
        scratch_shapes=[pltpu.VMEM((8, nl), f32)],
        compiler_params=_params(("arbitrary",)), name="router",
    )(logits)


ROW_DMA_UNROLL = 4


def _dispatch_body(slot_ref, h_ref, zero_ref, xs_ref, sem, *, tg, s):
    del zero_ref
    i = pl.program_id(0)

    def issue(r, carry):
        src = h_ref.at[pl.ds(pl.multiple_of(r * s, s), s), :]
        for kq in range(EXPERT_TOP_K):
            sl = slot_ref[(i * tg + r) * EXPERT_TOP_K + kq]
            pltpu.make_async_copy(src, xs_ref.at[pl.ds(pl.multiple_of(sl * s, s), s), :], sem).start()
        return carry

    lax.fori_loop(0, tg, issue, 0, unroll=ROW_DMA_UNROLL)
    for _ in range(EXPERT_TOP_K):
        pltpu.make_async_copy(h_ref, xs_ref.at[pl.ds(0, tg * s), :], sem).wait()


def _dispatch(h_rows, slot, n_slots, s, tg):
    n = h_rows.shape[0] // s
    tg = min(tg, n)
    assert n % tg == 0
    zeros = jnp.zeros((n_slots * s, LANES), f32)
    return pl.pallas_call(
        functools.partial(_dispatch_body, tg=tg, s=s),
        grid_spec=pltpu.PrefetchScalarGridSpec(
            num_scalar_prefetch=1, grid=(n // tg,),
            in_specs=[pl.BlockSpec((tg * s, LANES), lambda i, sl: (i, 0)), pl.BlockSpec(memory_space=pl.ANY)],
            out_specs=pl.BlockSpec(memory_space=pl.ANY),
            scratch_shapes=[pltpu.SemaphoreType.DMA(())]),
        out_shape=jax.ShapeDtypeStruct((n_slots * s, LANES), f32),
        input_output_aliases={2: 0},
        compiler_params=_params(("arbitrary",)), name="dispatch",
    )(slot, h_rows, zeros)


def _ffn_body(be_ref, nu_ref, x_ref, w1_ref, w3_ref, w2_ref, y_ref, xb_ref, w1b_ref, w3b_ref, w2b_ref, *, rb, s):
    i = pl.program_id(0)

    @pl.when(i < nu_ref[0])
    def _():
        @pl.when((i == 0) | (be_ref[i] != be_ref[jnp.maximum(i - 1, 0)]))
        def _():
            w1b_ref[...] = w1_ref[0].astype(bf16)
            w3b_ref[...] = w3_ref[0].astype(bf16)
            w2b_ref[...] = w2_ref[0].astype(bf16)

        for c in range(s):
            xb_ref[:, c * LANES:(c + 1) * LANES] = x_ref[pl.ds(c, rb, stride=s), :].astype(bf16)
        xb = xb_ref[...]
        a = jnp.dot(xb, w1b_ref[...], preferred_element_type=f32)
        g = jnp.dot(xb, w3b_ref[...], preferred_element_type=f32)
        mid = (a * _sigmoid(a) * g).astype(bf16)
        y = jnp.dot(mid, w2b_ref[...], preferred_element_type=f32)
        for c in range(s):
            y_ref[pl.ds(c, rb, stride=s), :] = y[:, c * LANES:(c + 1) * LANES]

    @pl.when(i >= nu_ref[0])
    def _():
        y_ref[...] = jnp.zeros_like(y_ref)


def _expert_ffn(xs_rows, blk_expert, n_used, w1, w3, w2, rb):
    n_exp, d, de = w1.shape
    s = d // LANES
    ns = xs_rows.shape[0] // s
    row = lambda i, be, nu: (i, 0)
    return pl.pallas_call(
        functools.partial(_ffn_body, rb=rb, s=s),
        grid_spec=pltpu.PrefetchScalarGridSpec(
            num_scalar_prefetch=2, grid=(ns // rb,),
            in_specs=[pl.BlockSpec((rb * s, LANES), row),
                      pl.BlockSpec((1, d, de), lambda i, be, nu: (be[i], 0, 0)),
                      pl.BlockSpec((1, d, de), lambda i, be, nu: (be[i], 0, 0)),
                      pl.BlockSpec((1, de, d), lambda i, be, nu: (be[i], 0, 0))],
            out_specs=pl.BlockSpec((rb * s, LANES), row),
            scratch_shapes=[pltpu.VMEM((rb, d), bf16), pltpu.VMEM((d, de), bf16), pltpu.VMEM((d, de), bf16),
                            pltpu.VMEM((de, d), bf16)]),
        out_shape=jax.ShapeDtypeStruct((ns * s, LANES), f32),
        compiler_params=_params(("arbitrary",)), name="expert_ffn",
    )(blk_expert, n_used, xs_rows, w1, w3, w2)


def _combine_body(slot_ref, x2_ref, rg_ref, gf_ref, ys_ref, o_ref, ybuf, sem, *, tc, s, n_steps):
    i = pl.program_id(0)
    cur = i % 2

    def start_gather(step, buf):
        def issue(r, carry):
            for kq in range(EXPERT_TOP_K):
                sl = slot_ref[(step * tc + r) * EXPERT_TOP_K + kq]
                pltpu.make_async_copy(ys_ref.at[pl.ds(pl.multiple_of(sl * s, s), s), :],
                                      ybuf.at[buf, kq, pl.ds(pl.multiple_of(r * s, s), s), :], sem.at[buf]).start()
            return carry

        lax.fori_loop(0, tc, issue, 0, unroll=ROW_DMA_UNROLL)

    @pl.when(i == 0)
    def _():
        start_gather(i, cur)

    @pl.when(i + 1 < n_steps)
    def _():
        start_gather(i + 1, 1 - cur)

    for kq in range(EXPERT_TOP_K):
        pltpu.make_async_copy(ys_ref.at[pl.ds(0, tc * s), :], ybuf.at[cur, kq], sem.at[cur]).wait()

    rg = rg_ref[...]
    lane = lax.broadcasted_iota(i32, rg.shape, 1)
    gates = [_round_bf16(jnp.sum(jnp.where(lane == kq, rg, 0.0), axis=-1, keepdims=True))
             for kq in range(EXPERT_TOP_K)]
    ss = jnp.zeros((tc, 1), f32)
    for c in range(s):
        cs = slice(c * LANES, (c + 1) * LANES)
        val = x2_ref[:, cs]
        for kq in range(EXPERT_TOP_K):
            val = val + gates[kq] * _round_bf16(ybuf[cur, kq, pl.ds(c, tc, stride=s), :])
        o_ref[:, cs] = val
        ss = ss + jnp.sum(val * val, axis=-1, keepdims=True)
    o_ref[...] = o_ref[...] * lax.rsqrt(ss * (1.0 / (s * LANES)) + RMS_EPS) * gf_ref[...]


def _combine(x2, gates, g_final, ys_rows, slot, tc):
    n, d = x2.shape
    s = d // LANES
    tc = min(tc, n)
    assert n % tc == 0
    return pl.pallas_call(
        functools.partial(_combine_body, tc=tc, s=s, n_steps=n // tc),
        grid_spec=pltpu.PrefetchScalarGridSpec(
            num_scalar_prefetch=1, grid=(n // tc,),
            in_specs=[pl.BlockSpec((tc, d), lambda i, sl: (i, 0)), pl.BlockSpec((tc, LANES), lambda i, sl: (i, 0)),
                      pl.BlockSpec((1, d), lambda i, sl: (0, 0)), pl.BlockSpec(memory_space=pl.ANY)],
            out_specs=pl.BlockSpec((tc, d), lambda i, sl: (i, 0)),
            scratch_shapes=[pltpu.VMEM((2, EXPERT_TOP_K, tc * s, LANES), f32), pltpu.SemaphoreType.DMA((2,))]),
        out_shape=jax.ShapeDtypeStruct((n, d), f32),
        compiler_params=_params(("arbitrary",)), name="combine",
    )(slot, x2, gates, g_final.reshape(1, d), ys_rows)


def _moe_and_final(x2, h_rows, logits, g_final, w1, w3, w2, n_grp, *, tm_route, rb, tg, tc):
    n, d = x2.shape
    n_exp = w1.shape[0]
    s = d // LANES
    ri, rg, cnt = _router(logits, n_exp, n_grp, tm_route)
    sizes = cnt[0, :n_exp].astype(i32)
    padded = (sizes + rb - 1) // rb * rb
    pad_end = jnp.cumsum(padded)
    pad_start = pad_end - padded
    slot = (pad_start[ri[:, :EXPERT_TOP_K]] + ri[:, EXPERT_TOP_K:2 * EXPERT_TOP_K]).reshape(-1).astype(i32)
    n_blocks = -(-(n * EXPERT_TOP_K) // rb) + n_exp
    blk_start = jnp.arange(n_blocks, dtype=i32) * rb
    blk_expert = jnp.minimum(jnp.sum((pad_end[None, :] <= blk_start[:, None]).astype(i32), axis=1), n_exp - 1)
    n_used = (pad_end[-1:] // rb).astype(i32)
    xs_rows = _dispatch(h_rows, slot, n_blocks * rb, s, tg)
    ys_rows = _expert_ffn(xs_rows, blk_expert, n_used, w1, w3, w2, rb)
    return _combine(x2, rg, g_final, ys_rows, slot, tc)


def _layer(l, xp, xs, cache_k, cache_v, cache_conv, cache_mem_k, cache_mem_v, page_table, mem_prompt, rel_table,
           g_mix, w_in, w_dw, b_dw, ln_g, ln_b, w_pw, w_out, g_mem, w_xk, w_xv, g_xattn, w_xq, w_xo, g_moe,
           w_rg, b_rg, w_re, b_re, w1, w3, w2, g_final, is_last):
    assert is_last, "the final norm is fused into the last layer's combine kernel"
    bp, s_len, d = xp.shape
    bs, t_len, _ = xs.shape
    page, n_heads, dh = cache_k.shape[2:]
    width, dc = w_dw.shape[1:]
    attn_w = n_heads * dh
    n_mem, xa_heads, xa_dh = cache_mem_k.shape[2:]
    xa_w = xa_heads * xa_dh
    n_grp, n_exp = w_rg.shape[-1], w_re.shape[-1]

    w_in_b = w_in[l].astype(bf16)
    w_pw_b, w_out_b = w_pw[l].astype(bf16), w_out[l].astype(bf16)
    w_xq_b, w_xo_b = w_xq[l].astype(bf16), w_xo[l].astype(bf16)
    w_mem_b = jnp.concatenate([w_xk[l], w_xv[l]], axis=1).astype(bf16)
    nl =-(-(n_exp + n_grp) // LANES) * LANES
    w_r = jnp.pad(jnp.concatenate([w_re[l], w_rg[l]], axis=1), ((0, 0), (0, nl - n_exp - n_grp))).astype(bf16)
    b_r = jnp.pad(jnp.concatenate([b_re[l], b_rg[l]]), (0, nl - n_exp - n_grp)).reshape(1, nl)
    pieces = [(0, 2 * dc), (2 * dc, attn_w), (2 * dc + attn_w, attn_w), (2 * dc + 2 * attn_w, attn_w),
              (2 * dc + 3 * attn_w, d), (2 * dc + 3 * attn_w + d, d)]

    n_p = bp * s_len
    tm_in = 512
    if _block_means_fit_under(n_p, tm_in, pieces, page_table.shape, page):
        (glu, q, k, v, gc, ga), bmeans = _norm_proj_with_block_means(xp.reshape(n_p, d), g_mix[l], w_in_b, pieces,
                                                                     tm_in, cache_k, l, page_table)
    else:
        glu, q, k, v, gc, ga = _norm_proj(xp.reshape(n_p, d), g_mix[l], w_in_b, pieces, tm=tm_in)
        bmeans = None
    cg, conv_p = _conv_branch(glu.reshape(bp, s_len, 2 * dc), gc.reshape(bp, s_len, d), None, w_dw[l], b_dw[l],
                              ln_g[l], ln_b[l], w_pw_b, ts=256)
    ag = _moba_prompt(q.reshape(bp, s_len, attn_w), k.reshape(bp, s_len, attn_w), v.reshape(bp, s_len, attn_w),
                      ga.reshape(bp, s_len, attn_w), rel_table, page)
    mk, mv = _norm_proj(mem_prompt.reshape(bp * n_mem, d), g_mem[l], w_mem_b, [(0, xa_w), (xa_w, xa_w)], tm=512)
    x2, h2, logits = _token_stage(cg.reshape(n_p, d), ag.reshape(n_p, d), xp.reshape(n_p, d),
                                  mk.reshape(bp, n_mem, xa_w), mv.reshape(bp, n_mem, xa_w), w_out_b, g_xattn[l],
                                  w_xq_b, w_xo_b, g_moe[l], w_r, b_r, tm=256, rows_per_seq=s_len,
                                  xa_heads=xa_heads)
    y_p = _moe_and_final(x2, h2, logits, g_final, w1[l], w3[l], w2[l], n_grp, tm_route=256, rb=256, tg=256, tc=256)
    prompt_out = (y_p.reshape(bp, s_len, d), k.reshape(bp, s_len, n_heads, dh), v.reshape(bp, s_len, n_heads, dh),
                  conv_p, mk.reshape(bp, n_mem, xa_heads, xa_dh), mv.reshape(bp, n_mem, xa_heads, xa_dh))

    rows = SAMPLE_ROWS
    assert t_len <= rows
    n_s = bs * rows
    xs_pad = jnp.pad(xs, ((0, 0), (0, rows - t_len), (0, 0))).reshape(n_s, d)
    glu, q, k, v, gc, ga = _norm_proj(xs_pad, g_mix[l], w_in_b, pieces, tm=512)
    state = jnp.pad(cache_conv[l], ((0, 0), (HALO - (width - 1), 0), (0, 0)))
    cg, conv_s = _conv_branch(glu.reshape(bs, rows, 2 * dc), gc.reshape(bs, rows, d), state, w_dw[l], b_dw[l],
                              ln_g[l], ln_b[l], w_pw_b, ts=rows, t_real=t_len, out_dtype=f32)
    ag = _moba_sample(q.reshape(bs, rows, attn_w), k.reshape(bs, rows, attn_w), v.reshape(bs, rows, attn_w),
                      ga.reshape(bs, rows, attn_w), cache_k, cache_v, l, page_table, rel_table, t_len, bmeans)
    x2, h2, logits = _token_stage(cg.reshape(n_s, d), ag.reshape(n_s, d), xs_pad,
                                  cache_mem_k[l].reshape(bs, n_mem, xa_w), cache_mem_v[l].reshape(bs, n_mem, xa_w),
                                  w_out_b, g_xattn[l], w_xq_b, w_xo_b, g_moe[l], w_r, b_r, tm=rows,
                                  rows_per_seq=rows, xa_heads=xa_heads)
    real = lambda a: a.reshape(bs, rows, -1)[:, :t_len].reshape(bs * t_len, -1)
    h2_real = h2.reshape(bs, rows, d)[:, :t_len].reshape(bs * t_len * (d // LANES), LANES)
    y_s = _moe_and_final(real(x2), h2_real, real(logits), g_final, w1[l], w3[l], w2[l], n_grp, tm_route=128, rb=16,
                         tg=128, tc=128)
    sample_out = (y_s.reshape(bs, t_len, d), real(k).reshape(bs, t_len, n_heads, dh),
                  real(v).reshape(bs, t_len, n_heads, dh), conv_s)
    return prompt_out, sample_out


def kernel(x_prompt, x_sample, cache_k, cache_v, cache_conv, cache_mem_k, cache_mem_v, page_table, mem_prompt, rel_table, g_mix, w_in, w_dw, b_dw, ln_g, ln_b, w_pw, w_out, g_mem, w_xk, w_xv, g_xattn, w_xq, w_xo, g_moe, w_rg, b_rg, w_re, b_re, w1, w3, w2, g_final):
    depth = w_in.shape[0]
    assert depth == 1, "one layer per step"
    (y_p, k_p, v_p, conv_p, mk_p, mv_p), (y_s, k_s, v_s, conv_s) = _layer(
        0, x_prompt, x_sample, cache_k, cache_v, cache_conv, cache_mem_k, cache_mem_v, page_table, mem_prompt,
        rel_table, g_mix, w_in, w_dw, b_dw, ln_g, ln_b, w_pw, w_out, g_mem, w_xk, w_xv, g_xattn, w_xq, w_xo, g_moe,
        w_rg, b_rg, w_re, b_re, w1, w3, w2, g_final, is_last=True)
    return (y_p, y_s, k_p[None], v_p[None], conv_p[None], mk_p[None], mv_p[None], k_s[None], v_s[None],
            conv_s[None])
```

```python
import functools
import math

import numpy as np
import jax
import jax.numpy as jnp
from jax import lax
from jax.experimental import pallas as pl
from jax.experimental.pallas import tpu as pltpu

f32 = jnp.float32
bf16 = jnp.bfloat16
i32 = jnp.int32

MOBA_BLOCK = 256
MOBA_TOP_K = 3
MAX_DISTANCE = 128
EXPERT_TOP_K = 2
RMS_EPS = 1e-6
LN_EPS = 1e-5

LANES = 128
SUBLANES = 8
SAMPLE_ROWS = 16
MASKED = -1e30
BIG_I = 1 << 20
LOG2E = 1.4426950408889634
VMEM_LIMIT = 56 * 1024 * 1024


def _params(sem, vmem=VMEM_LIMIT):
    return pltpu.CompilerParams(dimension_semantics=sem, vmem_limit_bytes=vmem)


def _sigmoid(x):
    return 1.0 / (1.0 + jnp.exp(-x))


def _rms(x, g, eps=RMS_EPS):
    return x * lax.rsqrt(jnp.mean(x * x, axis=-1, keepdims=True) + eps) * g


def _nt(a, b):
    return lax.dot_general(a, b, (((1,), (1,)), ((), ())), preferred_element_type=f32)


def _round_bf16(x):
    return x.astype(bf16).astype(f32)


def _t5_bucket_np(dist, n_buckets):
    dist = np.maximum(dist, 0)
    max_exact = n_buckets // 2
    ratio = (np.log(np.maximum(dist, 1).astype(np.float32) / np.float32(max_exact))
             / np.float32(math.log(MAX_DISTANCE / max_exact))).astype(np.float32)
    large = np.minimum(max_exact + (ratio * np.float32(n_buckets - max_exact)).astype(np.int32), n_buckets - 1)
    return np.where(dist < max_exact, dist, large).astype(np.int32)


def _norm_proj_body(x_ref, g_ref, *refs, n_out):
    w_refs, o_refs, hn_ref = refs[:n_out], refs[n_out:2 * n_out], refs[2 * n_out]

    @pl.when(pl.program_id(1) == 0)
    def _():
        hn_ref[...] = _rms(x_ref[...], g_ref[...]).astype(bf16)

    h = hn_ref[...]
    for w_ref, o_ref in zip(w_refs, o_refs):
        o_ref[...] = jnp.dot(h, w_ref[...], preferred_element_type=f32).astype(o_ref.dtype)


def _norm_proj_means_body(pt_ref, x_ref, g_ref, *refs, n_out, n_steps, layer, pps, ppb, page):
    w_refs, ck_ref = refs[:n_out], refs[n_out]
    o_refs, bm_ref = refs[n_out + 1:2 * n_out + 1], refs[2 * n_out + 1]
    pbuf, sem = refs[2 * n_out + 2:]
    step = pl.program_id(0) * pl.num_programs(1) + pl.program_id(1)
    slot = step % 2

    def copies(st, sl):
        return [pltpu.make_async_copy(ck_ref.at[layer, pt_ref[st * pps + p]], pbuf.at[sl, p], sem.at[sl])
                for p in range(pps)]

    @pl.when(step == 0)
    def _():
        for cp in copies(step, slot):
            cp.start()

    @pl.when(step + 1 < n_steps)
    def _():
        for cp in copies(step + 1, 1 - slot):
            cp.start()

    h = _rms(x_ref[...], g_ref[...]).astype(bf16)
    for w_ref, o_ref in zip(w_refs, o_refs):
        o_ref[...] = jnp.dot(h, w_ref[...], preferred_element_type=f32).astype(o_ref.dtype)

    for cp in copies(step, slot):
        cp.wait()
    for blk_i in range(pps // ppb):
        acc = jnp.zeros(bm_ref.shape[2:], f32)
        for p in range(ppb):
            acc = acc + jnp.sum(pbuf[slot, blk_i * ppb + p], axis=0) * (1.0 / page)
        bm_ref[0, blk_i] = acc * (1.0 / ppb)


def _norm_proj(x, g, w, pieces, tm):
    m, k = x.shape
    tm = min(tm, m)
    assert m % tm == 0
    min_w = min(wd for _, wd in pieces)
    nj = max(1, min_w // 256)
    in_specs = [pl.BlockSpec((tm, k), lambda i, j: (i, 0)), pl.BlockSpec((1, k), lambda i, j: (0, 0))]
    out_specs, out_shape = [], []
    for off, wd in pieces:
        assert wd % nj == 0 and off % (wd // nj) == 0
        tn = wd // nj
        in_specs.append(pl.BlockSpec((k, tn), functools.partial(lambda i, j, o: (0, o + j), o=off // tn)))
        out_specs.append(pl.BlockSpec((tm, tn), lambda i, j: (i, j)))
        out_shape.append(jax.ShapeDtypeStruct((m, wd), f32))
    return pl.pallas_call(
        functools.partial(_norm_proj_body, n_out=len(pieces)),
        grid=(m // tm, nj), in_specs=in_specs, out_specs=out_specs, out_shape=out_shape,
        scratch_shapes=[pltpu.VMEM((tm, k), bf16)],
        compiler_params=_params(("parallel", "arbitrary")), name="norm_proj",
    )(x, g.reshape(1, k), *([w] * len(pieces)))


def _block_means_fit_under(m, tm, pieces, page_table_shape, page):
    nj = min(wd for _, wd in pieces) // 256
    if nj < 1 or m % tm:
        return False
    bs, n_pages = page_table_shape
    n_steps = (m // tm) * nj
    if (bs * n_pages) % n_steps:
        return False
    pps = bs * n_pages // n_steps
    return pps % (MOBA_BLOCK // page) == 0 and n_pages % pps == 0


def _norm_proj_with_block_means(x, g, w, pieces, tm, cache_k, layer, page_table):
    m, k = x.shape
    assert m % tm == 0
    nj = min(wd for _, wd in pieces) // 256
    n_steps = (m // tm) * nj
    _, _, page, n_heads, dh = cache_k.shape
    bs, n_pages = page_table.shape
    ppb = MOBA_BLOCK // page
    n_blocks = n_pages // ppb
    assert n_pages % ppb == 0 and (bs * n_pages) % n_steps == 0
    pps = bs * n_pages // n_steps
    assert pps % ppb == 0 and n_pages % pps == 0
    bps, spb = pps // ppb, n_pages // pps
    ni = m // tm
    in_specs = [pl.BlockSpec((tm, k), lambda j, i, pt: (i, 0)), pl.BlockSpec((1, k), lambda j, i, pt: (0, 0))]
    out_specs, out_shape = [], []
    for off, wd in pieces:
        assert wd % nj == 0 and off % (wd // nj) == 0
        tn = wd // nj
        in_specs.append(pl.BlockSpec((k, tn), functools.partial(lambda j, i, pt, o: (0, o + j), o=off // tn)))
        out_specs.append(pl.BlockSpec((tm, tn), lambda j, i, pt: (i, j)))
        out_shape.append(jax.ShapeDtypeStruct((m, wd), f32))
    in_specs.append(pl.BlockSpec(memory_space=pl.ANY))
    out_specs.append(pl.BlockSpec((1, bps, n_heads, dh),
                                  lambda j, i, pt: ((j * ni + i) // spb, (j * ni + i) % spb, 0, 0)))
    out_shape.append(jax.ShapeDtypeStruct((bs, n_blocks, n_heads, dh), f32))
    outs = pl.pallas_call(
        functools.partial(_norm_proj_means_body, n_out=len(pieces), n_steps=n_steps, layer=layer, pps=pps,
                          ppb=ppb, page=page),
        grid_spec=pltpu.PrefetchScalarGridSpec(
            num_scalar_prefetch=1, grid=(nj, ni), in_specs=in_specs, out_specs=out_specs,
            scratch_shapes=[pltpu.VMEM((2, pps, page, n_heads, dh), f32), pltpu.SemaphoreType.DMA((2,))]),
        out_shape=out_shape,
        compiler_params=_params(("arbitrary", "arbitrary")), name="norm_proj_means",
    )(page_table.reshape(-1), x, g.reshape(1, k), *([w] * len(pieces)), cache_k)
    return outs[:-1], outs[-1]


HALO = 32


def _conv_body(glu_ref, halo_ref, gc_ref, wdw_ref, bdw_ref, lng_ref, lnb_ref, wpw_ref, cg_ref, st_ref,
               hist_ref, histr_ref, y_ref, *, dc, width, ts, t_real, halo_is_state, n_tiles):
    i = pl.program_id(1)
    glu = glu_ref[0]
    hist_ref[HALO:HALO + ts, :] = glu[:, :dc] * _sigmoid(glu[:, dc:])
    if halo_is_state:
        hist_ref[0:HALO, :] = halo_ref[0]
    else:
        hl = halo_ref[0]
        uh = hl[:, :dc] * _sigmoid(hl[:, dc:])
        hist_ref[0:HALO, :] = jnp.where(i == 0, 0.0, uh)
    histr_ref[0:HALO + ts, :] = _round_bf16(hist_ref[0:HALO + ts, :])
    histr_ref[HALO + ts:HALO + ts + SUBLANES, :] = jnp.zeros((SUBLANES, dc), f32)
    base = HALO - (width - 1)
    rs = min(ts, 128)
    for c in range(dc // LANES):
        cs = slice(c * LANES, (c + 1) * LANES)
        for r0 in range(0, ts, rs):
            acc = bdw_ref[:, cs]
            for r in range(SUBLANES):
                part = None
                for w in range(width):
                    if (base + w) % SUBLANES == r:
                        lo = r0 + base + w - r
                        term = histr_ref[lo:lo + rs + SUBLANES, cs] * wdw_ref[w:w + 1, cs]
                        part = term if part is None else part + term
                if part is not None:
                    acc = acc + part[r:r + rs]
            y_ref[r0:r0 + rs, cs] = acc
    y = y_ref[...]
    yc = y - jnp.mean(y, axis=-1, keepdims=True)
    var = jnp.mean(yc * yc, axis=-1, keepdims=True)
    z = yc * lax.rsqrt(var + LN_EPS) * lng_ref[...] + lnb_ref[...]
    z = z * _sigmoid(z)
    conv_out = jnp.dot(z.astype(bf16), wpw_ref[...], preferred_element_type=f32)
    cg_ref[0] = (_sigmoid(gc_ref[0]) * conv_out).astype(cg_ref.dtype)

    @pl.when(i == n_tiles - 1)
    def _():
        last = HALO + (ts if t_real is None else t_real)
        st_ref[0] = hist_ref[last - (width - 1):last, :]


def _conv_branch(glu, gc, state, w_dw, b_dw, ln_g, ln_b, w_pw, *, ts, t_real=None, out_dtype=bf16):
    b, t, _ = glu.shape
    width, dc = w_dw.shape
    d = w_pw.shape[1]
    ts = min(ts, t)
    assert t % ts == 0 and ts % HALO == 0 or t == ts
    assert width - 1 <= HALO and dc % LANES == 0
    n_tiles = t // ts
    halo_is_state = state is not None
    if halo_is_state:
        assert n_tiles == 1
        halo_arr = state
        halo_spec = pl.BlockSpec((1, HALO, dc), lambda bi, i: (bi, 0, 0))
    else:
        assert t >= width - 1
        halo_arr = glu
        r = ts // HALO
        halo_spec = pl.BlockSpec((1, HALO, 2 * dc), lambda bi, i: (bi, jnp.maximum(i * r - 1, 0), 0))
    const = lambda bi, i: (0, 0)
    return pl.pallas_call(
        functools.partial(_conv_body, dc=dc, width=width, ts=ts, t_real=t_real, halo_is_state=halo_is_state,
                          n_tiles=n_tiles),
        grid=(b, n_tiles),
        in_specs=[pl.BlockSpec((1, ts, 2 * dc), lambda bi, i: (bi, i, 0)), halo_spec,
                  pl.BlockSpec((1, ts, d), lambda bi, i: (bi, i, 0)),
                  pl.BlockSpec((width, dc), const), pl.BlockSpec((1, dc), const), pl.BlockSpec((1, dc), const),
                  pl.BlockSpec((1, dc), const), pl.BlockSpec((dc, d), const)],
        out_specs=[pl.BlockSpec((1, ts, d), lambda bi, i: (bi, i, 0)),
                   pl.BlockSpec((1, width - 1, dc), lambda bi, i: (bi, 0, 0))],
        out_shape=[jax.ShapeDtypeStruct((b, t, d), out_dtype), jax.ShapeDtypeStruct((b, width - 1, dc), f32)],
        scratch_shapes=[pltpu.VMEM((HALO + ts, dc), f32), pltpu.VMEM((HALO + ts + SUBLANES, dc), f32),
                        pltpu.VMEM((ts, dc), f32)],
        compiler_params=_params(("parallel", "arbitrary")), name="conv_branch",
    )(glu, halo_arr, gc, w_dw, b_dw.reshape(1, dc), ln_g.reshape(1, dc), ln_b.reshape(1, dc), w_pw)


def _bias_body(tab_ref, map_ref, o_ref, *, n_buckets):
    h = pl.program_id(0)
    bm = map_ref[...]
    acc = jnp.zeros(bm.shape, f32)
    for bk in range(n_buckets):
        acc = jnp.where(bm == bk, tab_ref[bk, h], acc)
    o_ref[0] = acc


def _bias_tiles(rel_table, bucket_map):
    n_buckets, n_heads = rel_table.shape
    r, c = bucket_map.shape
    return pl.pallas_call(
        functools.partial(_bias_body, n_buckets=n_buckets),
        grid=(n_heads,),
        in_specs=[pl.BlockSpec(memory_space=pltpu.SMEM), pl.BlockSpec((r, c), lambda h: (0, 0))],
        out_specs=pl.BlockSpec((1, r, c), lambda h: (h, 0, 0)),
        out_shape=jax.ShapeDtypeStruct((n_heads, r, c), f32),
        compiler_params=_params(("parallel",)), name="bias_tiles",
    )(rel_table, jnp.asarray(bucket_map))


def _moba_prompt_body(tab_ref, q_ref, k_ref, v_ref, ga_ref, bias_ref, o_ref, vt_ref, kb_ref, sel_ref, s_ref, *,
                      s_len, dh, page, n_buckets):
    blk = MOBA_BLOCK
    nb = s_len // blk
    ppb = blk // page
    h = pl.program_id(1)
    c2 = dh ** -0.5 * LOG2E
    far2 = tab_ref[n_buckets - 1, h] * LOG2E

    rows = []
    for n in range(nb):
        acc = jnp.zeros((1, dh), f32)
        for p in range(ppb):
            acc = acc + jnp.sum(k_ref[0, n * blk + p * page:n * blk + (p + 1) * page, :], axis=0,
                                keepdims=True) * (1.0 / page)
        rows.append(acc * (1.0 / ppb))
        vt_ref[n] = jnp.transpose(v_ref[0, n * blk:(n + 1) * blk, :]).astype(bf16)
        kb_ref[n] = k_ref[0, n * blk:(n + 1) * blk, :].astype(bf16)
    bmean = jnp.concatenate(rows, axis=0)
    sc = _nt(_round_bf16(bmean), _round_bf16(q_ref[0]))
    n_idx = lax.broadcasted_iota(i32, (nb, s_len), 0)
    own = lax.broadcasted_iota(i32, (nb, s_len), 1) // blk
    valid = n_idx < own
    scm = jnp.where(valid, sc, -jnp.inf)
    rank = jnp.zeros((nb, s_len), f32)
    for m in range(nb - 1):
        row = scm[m:m + 1, :]
        beats = (row > scm) | ((row == scm) & (m < n_idx))
        rank = rank + jnp.where(beats, 1.0, 0.0)
    sel_add = jnp.where(valid & (rank < MOBA_TOP_K), 0.0, MASKED)
    for j in range(nb):
        sel_ref[j] = sel_add[:, j * blk:(j + 1) * blk]

    kk = lax.broadcasted_iota(i32, (blk, blk), 0)
    qq = lax.broadcasted_iota(i32, (blk, blk), 1)
    own_b = jnp.where(kk <= qq, bias_ref[0, 0:blk, :] * LOG2E, MASKED)
    adj_b = bias_ref[0, blk:2 * blk, :] * LOG2E

    for j in range(nb):
        qj = q_ref[0, j * blk:(j + 1) * blk, :].astype(bf16)
        blocks = [j] + ([j - 1] if j >= 1 else []) + list(range(j - 1))
        m = None
        for t, n in enumerate(blocks):
            if n == j:
                add = own_b
            elif n == j - 1:
                add = adj_b + sel_ref[j, n:n + 1, :]
            else:
                add = far2 + sel_ref[j, n:n + 1, :]
            s = _nt(kb_ref[n], qj) * c2 + add
            s_ref[t] = s
            mt = jnp.max(s, axis=0, keepdims=True)
            m = mt if m is None else jnp.maximum(m, mt)
        l = jnp.zeros((1, blk), f32)
        acc = jnp.zeros((dh, blk), f32)
        for t, n in enumerate(blocks):
            p = jnp.exp2(s_ref[t] - m)
            l = l + jnp.sum(p, axis=0, keepdims=True)
            acc = acc + jnp.dot(vt_ref[n], p.astype(bf16), preferred_element_type=f32)
        out = jnp.transpose(acc * (1.0 / l))
        gate = _sigmoid(ga_ref[0, j * blk:(j + 1) * blk, :])
        o_ref[0, j * blk:(j + 1) * blk, :] = (gate * out).astype(o_ref.dtype)


def _moba_prompt(q, k, v, ga, rel_table, page):
    b, s_len, width = q.shape
    n_buckets, n_heads = rel_table.shape
    dh = width // n_heads
    blk = MOBA_BLOCK
    assert s_len % blk == 0 and blk % page == 0 and dh % LANES == 0
    assert blk + 1 >= MAX_DISTANCE, "blocks two or more away must share the last bucket"
    kk = np.arange(blk)[:, None]
    qq = np.arange(blk)[None, :]
    bmap = np.concatenate([_t5_bucket_np(qq - kk, n_buckets), _t5_bucket_np(qq - kk + blk, n_buckets)], axis=0)
    bias = _bias_tiles(rel_table, bmap)
    nb = s_len // blk
    qkv_spec = pl.BlockSpec((1, s_len, dh), lambda bi, hi: (bi, 0, hi))
    return pl.pallas_call(
        functools.partial(_moba_prompt_body, s_len=s_len, dh=dh, page=page, n_buckets=n_buckets),
        grid=(b, n_heads),
        in_specs=[pl.BlockSpec(memory_space=pltpu.SMEM), qkv_spec, qkv_spec, qkv_spec, qkv_spec,
                  pl.BlockSpec((1, 2 * blk, blk), lambda bi, hi: (hi, 0, 0))],
        out_specs=pl.BlockSpec((1, s_len, dh), lambda bi, hi: (bi, 0, hi)),
        out_shape=jax.ShapeDtypeStruct((b, s_len, width), bf16),
        scratch_shapes=[pltpu.VMEM((nb, dh, blk), bf16), pltpu.VMEM((nb, blk, dh), bf16),
                        pltpu.VMEM((nb, nb, blk), f32), pltpu.VMEM((nb, blk, blk), f32)],
        compiler_params=_params(("parallel", "parallel")), name="moba_prompt",
    )(rel_table, q, k, v, ga, bias)


def _page_means_body(pt_ref, *refs, n_in, page, ppb):
    page_refs, o_ref = refs[:n_in], refs[n_in]
    for blk_i in range(n_in // ppb):
        acc = jnp.zeros(o_ref.shape[2:], f32)
        for p in range(ppb):
            acc = acc + jnp.sum(page_refs[blk_i * ppb + p][...], axis=0) * (1.0 / page)
        o_ref[0, blk_i] = acc * (1.0 / ppb)


def _block_means(cache_k, layer, page_table, n_blocks):
    _, _, page, n_heads, dh = cache_k.shape
    bs, n_pages = page_table.shape
    ppb = MOBA_BLOCK // page
    bps = math.gcd(n_blocks, 4)
    n_in = bps * ppb

    def page_spec(p):
        return pl.BlockSpec((None, None, page, n_heads, dh),
                            lambda b, c, pt: (layer, pt[b * n_pages + c * n_in + p], 0, 0, 0))

    return pl.pallas_call(
        functools.partial(_page_means_body, n_in=n_in, page=page, ppb=ppb),
        grid_spec=pltpu.PrefetchScalarGridSpec(
            num_scalar_prefetch=1, grid=(bs, n_blocks // bps),
            in_specs=[page_spec(p) for p in range(n_in)],
            out_specs=pl.BlockSpec((1, bps, n_heads, dh), lambda b, c, pt: (b, c, 0, 0))),
        out_shape=jax.ShapeDtypeStruct((bs, n_blocks, n_heads, dh), f32),
        compiler_params=_params(("parallel", "parallel")), name="block_means",
    )(page_table.reshape(-1), *([cache_k] * n_in))


def _select_body(q_ref, bm_ref, o_ref, *, n_top, n_heads, dh):
    out_lane = lax.broadcasted_iota(i32, o_ref.shape[2:], 1)
    for h in range(n_heads):
        cs = slice(h * dh, (h + 1) * dh)
        sc = _nt(_round_bf16(q_ref[0, :, cs]), _round_bf16(bm_ref[0, :, cs]))
        lane = lax.broadcasted_iota(i32, sc.shape, 1)
        out = jnp.zeros(o_ref.shape[2:], i32)
        for kq in range(n_top):
            best = jnp.max(sc, axis=-1, keepdims=True)
            idx = jnp.min(jnp.where(sc == best, lane, BIG_I), axis=-1, keepdims=True)
            out = jnp.where(out_lane == kq, idx, out)
            sc = jnp.where(lane == idx, -jnp.inf, sc)
        o_ref[0, h] = out


def _select_blocks(q, bmeans, n_heads, n_top):
    bs, rows, width = q.shape
    dh = width // n_heads
    n_blocks = bmeans.shape[1]
    return pl.pallas_call(
        functools.partial(_select_body, n_top=n_top, n_heads=n_heads, dh=dh),
        grid=(bs,),
        in_specs=[pl.BlockSpec((1, rows, width), lambda b: (b, 0, 0)),
                  pl.BlockSpec((1, n_blocks, width), lambda b: (b, 0, 0))],
        out_specs=pl.BlockSpec((1, n_heads, rows, LANES), lambda b: (b, 0, 0, 0)),
        out_shape=jax.ShapeDtypeStruct((bs, n_heads, rows, LANES), i32),
        compiler_params=_params(("parallel",)), name="select_blocks",
    )(q, bmeans)


def _moba_sample_body(phys_ref, sel_ref, tab_ref, ck_ref, cv_ref, q_ref, kn_ref, vn_ref, ga_ref, adj_ref, own_ref,
                      o_ref, kbuf, vbuf, sem, *, layer, t_len, n_top, ppb, page, n_heads, n_full, n_buckets, dh,
                      n_steps):
    b, h = pl.program_id(0), pl.program_id(1)
    blk = ppb * page
    scale = dh ** -0.5
    far_bias = tab_ref[n_buckets - 1, h]
    n_pg = n_top * ppb
    step = b * n_heads + h
    slot = step % 2

    def copies(cb, ch, sl, t, c):
        pg = phys_ref[((cb * t_len + t) * n_heads + ch) * n_pg + c]
        dst = pl.ds(c * page, page)
        return (pltpu.make_async_copy(ck_ref.at[layer, pg, :, ch, :], kbuf.at[sl, t, dst, :], sem.at[sl, 0]),
                pltpu.make_async_copy(cv_ref.at[layer, pg, :, ch, :], vbuf.at[sl, t, dst, :], sem.at[sl, 1]))

    def start_all(cb, ch, sl):
        for t in range(t_len):
            for c in range(n_pg):
                ck, cv = copies(cb, ch, sl, t, c)
                ck.start()
                cv.start()

    @pl.when(step == 0)
    def _():
        start_all(b, h, slot)

    @pl.when(step + 1 < n_steps)
    def _():
        nxt = step + 1
        start_all(nxt // n_heads, nxt % n_heads, 1 - slot)

    for t in range(t_len):
        for c in range(n_pg):
            ck, cv = copies(b, h, slot, t, c)
            ck.wait()
            cv.wait()

    rows = q_ref.shape[1]
    row_id = lax.broadcasted_iota(i32, (rows, 1), 0)
    out = jnp.zeros((rows, dh), f32)
    kn = _round_bf16(kn_ref[0])
    vn = _round_bf16(vn_ref[0])
    for t in range(t_len):
        qt = _round_bf16(q_ref[0, t:t + 1, :])
        s_sel = jnp.sum(_round_bf16(kbuf[slot, t]) * qt, axis=-1, keepdims=True) * scale
        biases = []
        for c in range(n_top):
            sblk = sel_ref[((b * t_len + t) * n_heads + h) * n_top + c]
            biases.append(jnp.where(sblk == n_full - 1, adj_ref[0, t * blk:(t + 1) * blk, :], far_bias))
        s_sel = s_sel + jnp.concatenate(biases, axis=0)
        s_own = jnp.sum(kn * qt, axis=-1, keepdims=True) * scale + own_ref[0, t * rows:(t + 1) * rows, :]
        s_own = jnp.where(row_id <= t, s_own, MASKED)
        m = jnp.maximum(jnp.max(s_sel, axis=0, keepdims=True), jnp.max(s_own, axis=0, keepdims=True))
        p_sel = jnp.exp(s_sel - m)
        p_own = jnp.exp(s_own - m)
        l = jnp.sum(p_sel, axis=0, keepdims=True) + jnp.sum(p_own, axis=0, keepdims=True)
        p_sel = _round_bf16(p_sel / l)
        p_own = _round_bf16(p_own / l)
        o_t = (jnp.sum(p_sel * _round_bf16(vbuf[slot, t]), axis=0, keepdims=True)
               + jnp.sum(p_own * vn, axis=0, keepdims=True))
        out = jnp.where(row_id == t, o_t, out)
    o_ref[0] = (_sigmoid(ga_ref[0]) * out).astype(o_ref.dtype)


def _moba_sample(q, k_new, v_new, ga, cache_k, cache_v, layer, page_table, rel_table, t_len, bmeans=None):
    bs, rows, width = q.shape
    _, _, page, n_heads, dh = cache_k.shape
    n_buckets = rel_table.shape[0]
    n_pages = page_table.shape[1]
    blk = MOBA_BLOCK
    ppb = blk // page
    past = n_pages * page
    n_full = past // blk
    assert past % blk == 0, "the new tokens must start a fresh MoBA block"
    assert t_len <= rows and t_len <= blk // 2 and blk + 1 >= MAX_DISTANCE
    n_top = min(MOBA_TOP_K, n_full)
    assert n_top >= 1
    if bmeans is None:
        bmeans = _block_means(cache_k, layer, page_table, n_full)
    sel = _select_blocks(q, bmeans.reshape(bs, n_full, width), n_heads, n_top)
    sel = jnp.transpose(sel[:, :, :t_len, :n_top], (0, 2, 1, 3))
    logical = sel[..., None] * ppb + jnp.arange(ppb, dtype=i32)
    logical = logical.reshape(bs, t_len, n_heads, n_top * ppb)
    hit = jnp.arange(n_pages, dtype=i32)[None, None, None, None, :] == logical[..., None]
    phys = jnp.sum(jnp.where(hit, page_table[:, None, None, None, :], 0), axis=-1)
    tt = np.arange(8)[:, None]
    adj_map = _t5_bucket_np(blk + tt - np.arange(blk)[None, :], n_buckets)
    own_map = _t5_bucket_np(tt - np.arange(LANES)[None, :], n_buckets)
    tiles = _bias_tiles(rel_table, np.concatenate([adj_map, own_map], axis=1))
    adj = tiles[:, :t_len, :blk].reshape(n_heads, t_len * blk, 1)
    own = tiles[:, :t_len, blk:blk + rows].reshape(n_heads, t_len * rows, 1)
    tok_spec = pl.BlockSpec((1, rows, dh), lambda b, h, *_: (b, 0, h))
    return pl.pallas_call(
        functools.partial(_moba_sample_body, layer=layer, t_len=t_len, n_top=n_top, ppb=ppb, page=page,
                          n_heads=n_heads, n_full=n_full, n_buckets=n_buckets, dh=dh, n_steps=bs * n_heads),
        grid_spec=pltpu.PrefetchScalarGridSpec(
            num_scalar_prefetch=2, grid=(bs, n_heads),
            in_specs=[pl.BlockSpec(memory_space=pltpu.SMEM), pl.BlockSpec(memory_space=pl.ANY),
                      pl.BlockSpec(memory_space=pl.ANY), tok_spec, tok_spec, tok_spec, tok_spec,
                      pl.BlockSpec((1, t_len * blk, 1), lambda b, h, *_: (h, 0, 0)),
                      pl.BlockSpec((1, t_len * rows, 1), lambda b, h, *_: (h, 0, 0))],
            out_specs=pl.BlockSpec((1, rows, dh), lambda b, h, *_: (b, 0, h)),
            scratch_shapes=[pltpu.VMEM((2, t_len, n_top * blk, dh), f32),
                            pltpu.VMEM((2, t_len, n_top * blk, dh), f32), pltpu.SemaphoreType.DMA((2, 2))]),
        out_shape=jax.ShapeDtypeStruct((bs, rows, width), f32),
        compiler_params=_params(("arbitrary", "arbitrary")), name="moba_sample",
    )(phys.reshape(-1).astype(i32), sel.reshape(-1).astype(i32), rel_table, cache_k, cache_v, q, k_new, v_new, ga,
      adj, own)


def _token_body(cg_ref, ag_ref, x_ref, mk_ref, mv_ref, wout_ref, gx_ref, wxq_ref, wxo_ref, gm_ref, wr_ref, br_ref,
                x2_ref, h2_ref, lg_ref, *, xa_heads, xa_dh):
    merged = (cg_ref[...].astype(f32) + ag_ref[...].astype(f32)).astype(bf16)
    x1 = x_ref[...] + jnp.dot(merged, wout_ref[...], preferred_element_type=f32)
    hx = _rms(x1, gx_ref[...]).astype(bf16)
    qx = jnp.dot(hx, wxq_ref[...], preferred_element_type=f32)
    scale = xa_dh ** -0.5
    outs = []
    for hd in range(xa_heads):
        cs = slice(hd * xa_dh, (hd + 1) * xa_dh)
        s = _nt(qx[:, cs].astype(bf16), mk_ref[0, :, cs].astype(bf16)) * scale
        p = jnp.exp(s - jnp.max(s, axis=-1, keepdims=True))
        p = p / jnp.sum(p, axis=-1, keepdims=True)
        outs.append(jnp.dot(p.astype(bf16), mv_ref[0, :, cs].astype(bf16), preferred_element_type=f32))
    o = jnp.concatenate(outs, axis=-1).astype(bf16)
    x2 = x1 + jnp.dot(o, wxo_ref[...], preferred_element_type=f32)
    x2_ref[...] = x2
    h2 = _rms(x2, gm_ref[...])
    tm = h2.shape[0]
    s = h2.shape[1] // LANES
    for c in range(s):
        h2_ref[pl.ds(c, tm, stride=s), :] = h2[:, c * LANES:(c + 1) * LANES]
    lg_ref[...] = jnp.dot(h2.astype(bf16), wr_ref[...], preferred_element_type=f32) + br_ref[...]


def _token_stage(cg, ag, x, mem_k, mem_v, w_out, g_x, w_xq, w_xo, g_moe, w_r, b_r, *, tm, rows_per_seq, xa_heads):
    n, d = x.shape
    s = d // LANES
    xa_w = w_xq.shape[1]
    n_mem = mem_k.shape[1]
    tm = min(tm, rows_per_seq)
    assert rows_per_seq % tm == 0 and n % tm == 0
    per = rows_per_seq // tm
    row = lambda i: (i, 0)
    const = lambda i: (0, 0)
    mem_spec = pl.BlockSpec((1, n_mem, xa_w), lambda i: (i // per, 0, 0))
    nl = w_r.shape[1]
    return pl.pallas_call(
        functools.partial(_token_body, xa_heads=xa_heads, xa_dh=xa_w // xa_heads),
        grid=(n // tm,),
        in_specs=[pl.BlockSpec((tm, d), row), pl.BlockSpec((tm, d), row), pl.BlockSpec((tm, d), row), mem_spec,
                  mem_spec, pl.BlockSpec((d, d), const), pl.BlockSpec((1, d), const), pl.BlockSpec((d, xa_w), const),
                  pl.BlockSpec((xa_w, d), const), pl.BlockSpec((1, d), const), pl.BlockSpec((d, nl), const),
                  pl.BlockSpec((1, nl), const)],
        out_specs=[pl.BlockSpec((tm, d), row), pl.BlockSpec((tm * s, LANES), row), pl.BlockSpec((tm, nl), row)],
        out_shape=[jax.ShapeDtypeStruct((n, d), f32), jax.ShapeDtypeStruct((n * s, LANES), f32),
                   jax.ShapeDtypeStruct((n, nl), f32)],
        compiler_params=_params(("parallel",)), name="token_stage",
    )(cg, ag, x, mem_k, mem_v, w_out, g_x.reshape(1, d), w_xq, w_xo, g_moe.reshape(1, d), w_r, b_r)


def _router_body(lg_ref, ri_ref, rg_ref, cnt_ref, carry_ref, *, n_exp, n_grp):
    tm, nl = lg_ref.shape
    epg = n_exp // n_grp

    @pl.when(pl.program_id(0) == 0)
    def _():
        carry_ref[...] = jnp.zeros_like(carry_ref)

    lg = lg_ref[...]
    lane = lax.broadcasted_iota(i32, (tm, nl), 1)
    is_grp = (lane >= n_exp) & (lane < n_exp + n_grp)
    gl = jnp.where(is_grp, lg, -jnp.inf)
    gmax = jnp.max(gl, axis=-1, keepdims=True)
    g_top = jnp.min(jnp.where(gl == gmax, lane - n_exp, BIG_I), axis=-1, keepdims=True)
    g_prob = 1.0 / jnp.sum(jnp.where(is_grp, jnp.exp(lg - gmax), 0.0), axis=-1, keepdims=True)
    in_grp = (lane >= g_top * epg) & (lane < (g_top + 1) * epg)
    el = jnp.where(in_grp, lg, -jnp.inf)
    m1 = jnp.max(el, axis=-1, keepdims=True)
    e1 = jnp.min(jnp.where(el == m1, lane, BIG_I), axis=-1, keepdims=True)
    el2 = jnp.where(lane == e1, -jnp.inf, el)
    m2 = jnp.max(el2, axis=-1, keepdims=True)
    e2 = jnp.min(jnp.where(el2 == m2, lane, BIG_I), axis=-1, keepdims=True)
    r21 = jnp.exp(m2 - m1)
    p1 = 1.0 / (1.0 + r21)
    gate1 = g_prob * p1
    gate2 = g_prob * (r21 * p1)

    hit1 = lane == e1
    hit2 = lane == e2
    onehot = jnp.where(hit1 | hit2, 1.0, 0.0)
    rr = lax.broadcasted_iota(i32, (tm, tm), 0)
    cc = lax.broadcasted_iota(i32, (tm, tm), 1)
    lower = jnp.where(cc < rr, 1.0, 0.0).astype(bf16)
    before = jnp.dot(lower, onehot.astype(bf16), preferred_element_type=f32) + carry_ref[0:1, :]
    rank1 = jnp.sum(jnp.where(hit1, before, 0.0), axis=-1, keepdims=True).astype(i32)
    rank2 = jnp.sum(jnp.where(hit2, before, 0.0), axis=-1, keepdims=True).astype(i32)
    carry_ref[0:1, :] = carry_ref[0:1, :] + jnp.sum(onehot, axis=0, keepdims=True)

    ri = jnp.where(lane == 0, e1, jnp.where(lane == 1, e2, jnp.where(lane == 2, rank1, jnp.where(lane == 3, rank2, 0))))
    ri_ref[...] = ri
    rg_ref[...] = jnp.where(lane == 0, gate1, jnp.where(lane == 1, gate2, 0.0))
    cnt_ref[...] = carry_ref[...]


def _router(logits, n_exp, n_grp, tm):
    n, nl = logits.shape
    tm = min(tm, n)
    assert n % tm == 0
    return pl.pallas_call(
        functools.partial(_router_body, n_exp=n_exp, n_grp=n_grp),
        grid=(n // tm,),
        in_specs=[pl.BlockSpec((tm, nl), lambda i: (i, 0))],
        out_specs=[pl.BlockSpec((tm, nl), lambda i: (i, 0)), pl.BlockSpec((tm, nl), lambda i: (i, 0)),
                   pl.BlockSpec((8, nl), lambda i: (0, 0))],
        out_shape=[jax.ShapeDtypeStruct((n, nl), i32), jax.ShapeDtypeStruct((n, nl), f32),
                   jax.ShapeDtypeStruct((8, nl), f32)],
        scratch_shapes=[pltpu.VMEM((8, nl), f32)],
        compiler_params=_params(("arbitrary",)), name="router",
    )(logits)


ROW_DMA_UNROLL = 4


def _dispatch_body(slot_ref, h_ref, zero_ref, xs_ref, sem, *, tg, s):
    del zero_ref
    i = pl.program_id(0)

    def issue(r, carry):
        src = h_ref.at[pl.ds(pl.multiple_of(r * s, s), s), :]
        for kq in range(EXPERT_TOP_K):
            sl = slot_ref[(i * tg + r) * EXPERT_TOP_K + kq]
            pltpu.make_async_copy(src, xs_ref.at[pl.ds(pl.multiple_of(sl * s, s), s), :], sem).start()
        return carry

    lax.fori_loop(0, tg, issue, 0, unroll=ROW_DMA_UNROLL)
    for _ in range(EXPERT_TOP_K):
        pltpu.make_async_copy(h_ref, xs_ref.at[pl.ds(0, tg * s), :], sem).wait()


def _dispatch(h_rows, slot, n_slots, s, tg):
    n = h_rows.shape[0] // s
    tg = min(tg, n)
    assert n % tg == 0
    zeros = jnp.zeros((n_slots * s, LANES), f32)
    return pl.pallas_call(
        functools.partial(_dispatch_body, tg=tg, s=s),
        grid_spec=pltpu.PrefetchScalarGridSpec(
            num_scalar_prefetch=1, grid=(n // tg,),
            in_specs=[pl.BlockSpec((tg * s, LANES), lambda i, sl: (i, 0)), pl.BlockSpec(memory_space=pl.ANY)],
            out_specs=pl.BlockSpec(memory_space=pl.ANY),
            scratch_shapes=[pltpu.SemaphoreType.DMA(())]),
        out_shape=jax.ShapeDtypeStruct((n_slots * s, LANES), f32),
        input_output_aliases={2: 0},
        compiler_params=_params(("arbitrary",)), name="dispatch",
    )(slot, h_rows, zeros)


def _ffn_body(be_ref, nu_ref, x_ref, w1_ref, w3_ref, w2_ref, y_ref, xb_ref, w1b_ref, w3b_ref, w2b_ref, *, rb, s):
    i = pl.program_id(0)

    @pl.when(i < nu_ref[0])
    def _():
        @pl.when((i == 0) | (be_ref[i] != be_ref[jnp.maximum(i - 1, 0)]))
        def _():
            w1b_ref[...] = w1_ref[0].astype(bf16)
            w3b_ref[...] = w3_ref[0].astype(bf16)
            w2b_ref[...] = w2_ref[0].astype(bf16)

        for c in range(s):
            xb_ref[:, c * LANES:(c + 1) * LANES] = x_ref[pl.ds(c, rb, stride=s), :].astype(bf16)
        xb = xb_ref[...]
        a = jnp.dot(xb, w1b_ref[...], preferred_element_type=f32)
        g = jnp.dot(xb, w3b_ref[...], preferred_element_type=f32)
        mid = (a * _sigmoid(a) * g).astype(bf16)
        y = jnp.dot(mid, w2b_ref[...], preferred_element_type=f32)
        for c in range(s):
            y_ref[pl.ds(c, rb, stride=s), :] = y[:, c * LANES:(c + 1) * LANES]

    @pl.when(i >= nu_ref[0])
    def _():
        y_ref[...] = jnp.zeros_like(y_ref)


def _expert_ffn(xs_rows, blk_expert, n_used, w1, w3, w2, rb):
    n_exp, d, de = w1.shape
    s = d // LANES
    ns = xs_rows.shape[0] // s
    row = lambda i, be, nu: (i, 0)
    return pl.pallas_call(
        functools.partial(_ffn_body, rb=rb, s=s),
        grid_spec=pltpu.PrefetchScalarGridSpec(
            num_scalar_prefetch=2, grid=(ns // rb,),
            in_specs=[pl.BlockSpec((rb * s, LANES), row),
                      pl.BlockSpec((1, d, de), lambda i, be, nu: (be[i], 0, 0)),
                      pl.BlockSpec((1, d, de), lambda i, be, nu: (be[i], 0, 0)),
                      pl.BlockSpec((1, de, d), lambda i, be, nu: (be[i], 0, 0))],
            out_specs=pl.BlockSpec((rb * s, LANES), row),
            scratch_shapes=[pltpu.VMEM((rb, d), bf16), pltpu.VMEM((d, de), bf16), pltpu.VMEM((d, de), bf16),
                            pltpu.VMEM((de, d), bf16)]),
        out_shape=jax.ShapeDtypeStruct((ns * s, LANES), f32),
        compiler_params=_params(("arbitrary",)), name="expert_ffn",
    )(blk_expert, n_used, xs_rows, w1, w3, w2)


def _combine_body(slot_ref, x2_ref, rg_ref, gf_ref, ys_ref, o_ref, ybuf, sem, *, tc, s, n_steps):
    i = pl.program_id(0)
    cur = i % 2

    def start_gather(step, buf):
        def issue(r, carry):
            for kq in range(EXPERT_TOP_K):
                sl = slot_ref[(step * tc + r) * EXPERT_TOP_K + kq]
                pltpu.make_async_copy(ys_ref.at[pl.ds(pl.multiple_of(sl * s, s), s), :],
                                      ybuf.at[buf, kq, pl.ds(pl.multiple_of(r * s, s), s), :], sem.at[buf]).start()
            return carry

        lax.fori_loop(0, tc, issue, 0, unroll=ROW_DMA_UNROLL)

    @pl.when(i == 0)
    def _():
        start_gather(i, cur)

    @pl.when(i + 1 < n_steps)
    def _():
        start_gather(i + 1, 1 - cur)

    for kq in range(EXPERT_TOP_K):
        pltpu.make_async_copy(ys_ref.at[pl.ds(0, tc * s), :], ybuf.at[cur, kq], sem.at[cur]).wait()

    rg = rg_ref[...]
    lane = lax.broadcasted_iota(i32, rg.shape, 1)
    gates = [_round_bf16(jnp.sum(jnp.where(lane == kq, rg, 0.0), axis=-1, keepdims=True))
             for kq in range(EXPERT_TOP_K)]
    ss = jnp.zeros((tc, 1), f32)
    for c in range(s):
        cs = slice(c * LANES, (c + 1) * LANES)
        val = x2_ref[:, cs]
        for kq in range(EXPERT_TOP_K):
            val = val + gates[kq] * _round_bf16(ybuf[cur, kq, pl.ds(c, tc, stride=s), :])
        o_ref[:, cs] = val
        ss = ss + jnp.sum(val * val, axis=-1, keepdims=True)
    o_ref[...] = o_ref[...] * lax.rsqrt(ss * (1.0 / (s * LANES)) + RMS_EPS) * gf_ref[...]


def _combine(x2, gates, g_final, ys_rows, slot, tc):
    n, d = x2.shape
    s = d // LANES
    tc = min(tc, n)
    assert n % tc == 0
    return pl.pallas_call(
        functools.partial(_combine_body, tc=tc, s=s, n_steps=n // tc),
        grid_spec=pltpu.PrefetchScalarGridSpec(
            num_scalar_prefetch=1, grid=(n // tc,),
            in_specs=[pl.BlockSpec((tc, d), lambda i, sl: (i, 0)), pl.BlockSpec((tc, LANES), lambda i, sl: (i, 0)),
                      pl.BlockSpec((1, d), lambda i, sl: (0, 0)), pl.BlockSpec(memory_space=pl.ANY)],
            out_specs=pl.BlockSpec((tc, d), lambda i, sl: (i, 0)),
            scratch_shapes=[pltpu.VMEM((2, EXPERT_TOP_K, tc * s, LANES), f32), pltpu.SemaphoreType.DMA((2,))]),
        out_shape=jax.ShapeDtypeStruct((n, d), f32),
        compiler_params=_params(("arbitrary",)), name="combine",
    )(slot, x2, gates, g_final.reshape(1, d), ys_rows)


def _moe_and_final(x2, h_rows, logits, g_final, w1, w3, w2, n_grp, *, tm_route, rb, tg, tc):
    n, d = x2.shape
    n_exp = w1.shape[0]
    s = d // LANES
    ri, rg, cnt = _router(logits, n_exp, n_grp, tm_route)
    sizes = cnt[0, :n_exp].astype(i32)
    padded = (sizes + rb - 1) // rb * rb
    pad_end = jnp.cumsum(padded)
    pad_start = pad_end - padded
    e_idx = ri[:, :EXPERT_TOP_K]
    before = jnp.arange(n_exp, dtype=i32)[None, None, :] < e_idx[:, :, None]
    first = jnp.sum(jnp.where(before, padded[None, None, :], 0), axis=-1)
    slot = (first + ri[:, EXPERT_TOP_K:2 * EXPERT_TOP_K]).reshape(-1).astype(i32)
    n_blocks = -(-(n * EXPERT_TOP_K) // rb) + n_exp
    blk_start = jnp.arange(n_blocks, dtype=i32) * rb
    blk_expert = jnp.minimum(jnp.sum((pad_end[None, :] <= blk_start[:, None]).astype(i32), axis=1), n_exp - 1)
    n_used = (pad_end[-1:] // rb).astype(i32)
    xs_rows = _dispatch(h_rows, slot, n_blocks * rb, s, tg)
    ys_rows = _expert_ffn(xs_rows, blk_expert, n_used, w1, w3, w2, rb)
    return _combine(x2, rg, g_final, ys_rows, slot, tc)


def _layer(l, xp, xs, cache_k, cache_v, cache_conv, cache_mem_k, cache_mem_v, page_table, mem_prompt, rel_table,
           g_mix, w_in, w_dw, b_dw, ln_g, ln_b, w_pw, w_out, g_mem, w_xk, w_xv, g_xattn, w_xq, w_xo, g_moe,
           w_rg, b_rg, w_re, b_re, w1, w3, w2, g_final, is_last):
    assert is_last, "the final norm is fused into the last layer's combine kernel"
    bp, s_len, d = xp.shape
    bs, t_len, _ = xs.shape
    page, n_heads, dh = cache_k.shape[2:]
    width, dc = w_dw.shape[1:]
    attn_w = n_heads * dh
    n_mem, xa_heads, xa_dh = cache_mem_k.shape[2:]
    xa_w = xa_heads * xa_dh
    n_grp, n_exp = w_rg.shape[-1], w_re.shape[-1]

    w_in_b = w_in[l].astype(bf16)
    w_pw_b, w_out_b = w_pw[l].astype(bf16), w_out[l].astype(bf16)
    w_xq_b, w_xo_b = w_xq[l].astype(bf16), w_xo[l].astype(bf16)
    w_mem_b = jnp.concatenate([w_xk[l], w_xv[l]], axis=1).astype(bf16)
    nl =-(-(n_exp + n_grp) // LANES) * LANES
    w_r = jnp.pad(jnp.concatenate([w_re[l], w_rg[l]], axis=1), ((0, 0), (0, nl - n_exp - n_grp))).astype(bf16)
    b_r = jnp.pad(jnp.concatenate([b_re[l], b_rg[l]]), (0, nl - n_exp - n_grp)).reshape(1, nl)
    pieces = [(0, 2 * dc), (2 * dc, attn_w), (2 * dc + attn_w, attn_w), (2 * dc + 2 * attn_w, attn_w),
              (2 * dc + 3 * attn_w, d), (2 * dc + 3 * attn_w + d, d)]

    n_p = bp * s_len
    tm_in = 512
    if _block_means_fit_under(n_p, tm_in, pieces, page_table.shape, page):
        (glu, q, k, v, gc, ga), bmeans = _norm_proj_with_block_means(xp.reshape(n_p, d), g_mix[l], w_in_b, pieces,
                                                                     tm_in, cache_k, l, page_table)
    else:
        glu, q, k, v, gc, ga = _norm_proj(xp.reshape(n_p, d), g_mix[l], w_in_b, pieces, tm=tm_in)
        bmeans = None
    cg, conv_p = _conv_branch(glu.reshape(bp, s_len, 2 * dc), gc.reshape(bp, s_len, d), None, w_dw[l], b_dw[l],
                              ln_g[l], ln_b[l], w_pw_b, ts=256)
    ag = _moba_prompt(q.reshape(bp, s_len, attn_w), k.reshape(bp, s_len, attn_w), v.reshape(bp, s_len, attn_w),
                      ga.reshape(bp, s_len, attn_w), rel_table, page)
    mk, mv = _norm_proj(mem_prompt.reshape(bp * n_mem, d), g_mem[l], w_mem_b, [(0, xa_w), (xa_w, xa_w)], tm=512)
    x2, h2, logits = _token_stage(cg.reshape(n_p, d), ag.reshape(n_p, d), xp.reshape(n_p, d),
                                  mk.reshape(bp, n_mem, xa_w), mv.reshape(bp, n_mem, xa_w), w_out_b, g_xattn[l],
                                  w_xq_b, w_xo_b, g_moe[l], w_r, b_r, tm=256, rows_per_seq=s_len,
                                  xa_heads=xa_heads)
    y_p = _moe_and_final(x2, h2, logits, g_final, w1[l], w3[l], w2[l], n_grp, tm_route=256, rb=256, tg=256, tc=256)
    prompt_out = (y_p.reshape(bp, s_len, d), k.reshape(bp, s_len, n_heads, dh), v.reshape(bp, s_len, n_heads, dh),
                  conv_p, mk.reshape(bp, n_mem, xa_heads, xa_dh), mv.reshape(bp, n_mem, xa_heads, xa_dh))

    rows = SAMPLE_ROWS
    assert t_len <= rows
    n_s = bs * rows
    xs_pad = jnp.pad(xs, ((0, 0), (0, rows - t_len), (0, 0))).reshape(n_s, d)
    glu, q, k, v, gc, ga = _norm_proj(xs_pad, g_mix[l], w_in_b, pieces, tm=512)
    state = jnp.pad(cache_conv[l], ((0, 0), (HALO - (width - 1), 0), (0, 0)))
    cg, conv_s = _conv_branch(glu.reshape(bs, rows, 2 * dc), gc.reshape(bs, rows, d), state, w_dw[l], b_dw[l],
                              ln_g[l], ln_b[l], w_pw_b, ts=rows, t_real=t_len, out_dtype=f32)
    ag = _moba_sample(q.reshape(bs, rows, attn_w), k.reshape(bs, rows, attn_w), v.reshape(bs, rows, attn_w),
                      ga.reshape(bs, rows, attn_w), cache_k, cache_v, l, page_table, rel_table, t_len, bmeans)
    x2, h2, logits = _token_stage(cg.reshape(n_s, d), ag.reshape(n_s, d), xs_pad,
                                  cache_mem_k[l].reshape(bs, n_mem, xa_w), cache_mem_v[l].reshape(bs, n_mem, xa_w),
                                  w_out_b, g_xattn[l], w_xq_b, w_xo_b, g_moe[l], w_r, b_r, tm=rows,
                                  rows_per_seq=rows, xa_heads=xa_heads)
    real = lambda a: a.reshape(bs, rows, -1)[:, :t_len].reshape(bs * t_len, -1)
    h2_real = h2.reshape(bs, rows, d)[:, :t_len].reshape(bs * t_len * (d // LANES), LANES)
    y_s = _moe_and_final(real(x2), h2_real, real(logits), g_final, w1[l], w3[l], w2[l], n_grp, tm_route=128, rb=16,
                         tg=128, tc=128)
    sample_out = (y_s.reshape(bs, t_len, d), real(k).reshape(bs, t_len, n_heads, dh),
                  real(v).reshape(bs, t_len, n_heads, dh), conv_s)
    return prompt_out, sample_out


def kernel(x_prompt, x_sample, cache_k, cache_v, cache_conv, cache_mem_k, cache_mem_v, page_table, mem_prompt, rel_table, g_mix, w_in, w_dw, b_dw, ln_g, ln_b, w_pw, w_out, g_mem, w_xk, w_xv, g_xattn, w_xq, w_xo, g_moe, w_rg, b_rg, w_re, b_re, w1, w3, w2, g_final):
    depth = w_in.shape[0]
    assert depth == 1, "one layer per step"
    (y_p, k_p, v_p, conv_p, mk_p, mv_p), (y_s, k_s, v_s, conv_s) = _layer(
        0, x_prompt, x_sample, cache_k, cache_v, cache_conv, cache_mem_k, cache_mem_v, page_table, mem_prompt,
        rel_table, g_mix, w_in, w_dw, b_dw, ln_g, ln_b, w_pw, w_out, g_mem, w_xk, w_xv, g_xattn, w_xq, w_xo, g_moe,
        w_rg, b_rg, w_re, b_re, w1, w3, w2, g_final, is_last=True)
    return (y_p, y_s, k_p[None], v_p[None], conv_p[None], mk_p[None], mv_p[None], k_s[None], v_s[None],
            conv_s[None])
```
